```python
import math, functools
import jax, jax.numpy as jnp
from jax import lax
import numpy as np

D_MODEL = 1024
BATCH = 16
SEQ = 256
DEPTH = 4
DEC_BATCH = 8
DEC_SEQ = 4096
PAST_LEN = 256

GRID_W = 64
D_HYENA = 512
D_LRU = 512
D_MIX = D_HYENA + D_LRU
N_IN = 3 * D_HYENA + 2 * D_LRU
HYENA_SHORT = 3
LRU_CONV = 4
N_LRU_HEADS = 8
LRU_HEAD_DIM = D_LRU // N_LRU_HEADS
LRU_C = 8.0
FILTER_HIDDEN = 64
N_BANDS = 4
POS_DIM = 1 + 2 * N_BANDS + 3
HYENA_DECAY_FAST = 0.3
HYENA_DECAY_SLOW = 1.5
HYENA_DECAY_TARGET = 1e-2
HYENA_SHIFT = 0.05
D_FF = 2816
N_EXPERTS = 8
TOP_K = 2
EPS = 1e-6
N_DENSE = (DEPTH + 1) // 2
N_MOE = DEPTH // 2

kernel_name = 'hyena_rglru_flow_step'


def _rmsnorm(x, g):
    xf = x.astype(jnp.float32)
    y = xf * lax.rsqrt(jnp.mean(xf * xf, axis=-1, keepdims=True) + EPS)
    return (y * g.astype(jnp.float32)).astype(x.dtype)


def _ada_mods(cvec, w, b):
    m = jax.nn.silu(cvec) @ w + b
    return [t[:, None, :] for t in jnp.split(m, 6, axis=-1)]


def _dwconv(u, w, b, left, right):
    L = u.shape[1]
    up = jnp.pad(u, ((0, 0), (left, right), (0, 0)))
    out = b
    for k in range(w.shape[0]):
        out = out + up[:, k:k + L] * w[k]
    return out


def _hyena_deltas():
    min_decay = math.log(HYENA_DECAY_TARGET) / HYENA_DECAY_SLOW
    max_decay = math.log(HYENA_DECAY_TARGET) / HYENA_DECAY_FAST
    return jnp.abs(jnp.linspace(min_decay, max_decay, D_HYENA, dtype=jnp.float32))


def _hyena_filter(L, grid, w1, b1, w2, b2, w3, b3, freq):
    d = jnp.arange(L, dtype=jnp.int32)
    t = d.astype(jnp.float32) / L
    bands = jnp.arange(1, N_BANDS + 1, dtype=jnp.float32)
    ang = 2.0 * math.pi * t[:, None] * bands[None, :]
    feats = [t[:, None], jnp.sin(ang), jnp.cos(ang)]
    if grid:
        rows = max(L // GRID_W, 1)
        col_ang = 2.0 * math.pi * (d % GRID_W).astype(jnp.float32) / GRID_W
        row = (d // GRID_W).astype(jnp.float32) / rows
        feats += [jnp.sin(col_ang)[:, None], jnp.cos(col_ang)[:, None], row[:, None]]
    else:
        feats.append(jnp.zeros((L, 3), jnp.float32))
    z = jnp.concatenate(feats, axis=-1)
    hdn = jnp.sin(freq[0] * (z @ w1 + b1))
    hdn = jnp.sin(freq[1] * (hdn @ w2 + b2))
    k = (hdn @ w3 + b3).astype(jnp.float32).reshape(L, 2, D_HYENA)
    window = jnp.exp(-t[:, None] * _hyena_deltas()[None, :]) + HYENA_SHIFT
    k = k * window[:, None, :]
    kfull = jnp.concatenate([k[:, 0], jnp.zeros((1, D_HYENA), jnp.float32), jnp.flip(k[1:, 1], axis=0)], axis=0)
    return kfull / (jnp.sum(jnp.abs(kfull), axis=0, keepdims=True) + EPS)


def _bidir_fftconv(u, kfull, bias):
    L = u.shape[1]
    uf = u.astype(jnp.float32)
    U = jnp.fft.rfft(uf, n=2 * L, axis=1)
    K = jnp.fft.rfft(kfull, n=2 * L, axis=0)
    y = jnp.fft.irfft(U * K[None], n=2 * L, axis=1)[:, :L]
    return y + uf * bias.astype(jnp.float32)


def _hyena(proj, grid, p):
    z = _dwconv(proj, p['short_w'], p['short_b'], 1, 1)
    v, x1, x2 = jnp.split(z, 3, axis=-1)
    kfull = _hyena_filter(proj.shape[1], grid, p['filt_w1'], p['filt_b1'], p['filt_w2'], p['filt_b2'],
                          p['filt_w3'], p['filt_b3'], p['filt_freq'])
    y = x2.astype(jnp.float32) * _bidir_fftconv(v * x1, kfull, p['hyena_bias'])
    return y.astype(proj.dtype)


def _combine(e1, e2):
    a1, b1 = e1
    a2, b2 = e2
    return a1 * a2, a2 * b1 + b2


def _rglru(px, pg, h0, p):
    B, L, _ = px.shape
    xr = _dwconv(px, p['lru_conv_w'], p['lru_conv_b'], 2, 1).astype(jnp.float32)
    xh = xr.reshape(B, L, N_LRU_HEADS, LRU_HEAD_DIM)
    ra = jnp.einsum('blhi,dhij->dblhj', xh, p['lru_wa'].astype(jnp.float32)).reshape(2, B, L, D_LRU)
    rx = jnp.einsum('blhi,dhij->dblhj', xh, p['lru_wx'].astype(jnp.float32)).reshape(2, B, L, D_LRU)
    gate_a = jax.nn.sigmoid(ra + p['lru_ba'].astype(jnp.float32)[:, None, None, :])
    gate_x = jax.nn.sigmoid(rx + p['lru_bx'].astype(jnp.float32)[:, None, None, :])
    log_a = -LRU_C * gate_a * jax.nn.softplus(-p['lru_lambda'].astype(jnp.float32))[:, None, None, :]
    a = jnp.exp(log_a)
    mult = jnp.sqrt(jnp.maximum(-jnp.expm1(2.0 * log_a), 0.0))
    bx = xr[None] * gate_x * mult
    h0 = h0.astype(jnp.float32)
    a_f, b_f = lax.associative_scan(_combine, (a[0], bx[0]), axis=1)
    h_f = a_f * h0[:, 0, None, :] + b_f
    a_r, b_r = lax.associative_scan(_combine, (jnp.flip(a[1], axis=1), jnp.flip(bx[1], axis=1)), axis=1)
    h_r = a_r * h0[:, 1, None, :] + b_r
    h_b = jnp.flip(h_r, axis=1)
    y = (h_f + h_b) * jax.nn.gelu(pg.astype(jnp.float32))
    h_last = jnp.stack([h_f[:, -1], h_r[:, -1]], axis=1)
    return y.astype(px.dtype), h_last


def _swiglu(h, w_gu, w_down):
    g, u = jnp.split(h @ w_gu, 2, axis=-1)
    return (jax.nn.silu(g) * u) @ w_down


def _moe(h, router, w_gu, w_down):
    logits = (h @ router).astype(jnp.float32)
    top_v, top_i = lax.top_k(logits, TOP_K)
    top_w = jax.nn.softmax(top_v, axis=-1)
    comb = jnp.sum(jax.nn.one_hot(top_i, N_EXPERTS, dtype=jnp.float32) * top_w[..., None], axis=-2)
    out = jnp.zeros(h.shape, jnp.float32)
    for e in range(N_EXPERTS):
        out = out + comb[..., e:e + 1] * _swiglu(h, w_gu[e], w_down[e]).astype(jnp.float32)
    return out.astype(h.dtype)


def _layer(x, cvec, h0, grid, p, ffn):
    sh1, sc1, g1, sh2, sc2, g2 = _ada_mods(cvec, p['ada_w'], p['ada_b'])
    h = _rmsnorm(x, p['norm_mix']) * (1.0 + sc1) + sh1
    proj = h @ p['w_in']
    hy = _hyena(proj[..., :3 * D_HYENA], grid, p)
    lr, h_last = _rglru(proj[..., 3 * D_HYENA:3 * D_HYENA + D_LRU], proj[..., 3 * D_HYENA + D_LRU:], h0, p)
    mixed = jnp.concatenate([hy, lr], axis=-1) @ p['w_out']
    x = x + (g1 * mixed).astype(x.dtype)
    h = _rmsnorm(x, p['norm_ffn']) * (1.0 + sc2) + sh2
    x = x + (g2 * ffn(h)).astype(x.dtype)
    return x, h_last


def setup_inputs(seed: int = 0) -> dict:
    key = jax.random.key(seed)
    ks = jax.random.split(key, 40)
    f32 = jnp.float32

    def nrm(i, shape, scale):
        return jax.random.normal(ks[i], shape, f32) * scale

    a0 = jax.random.uniform(ks[28], (DEPTH, 2, D_LRU), f32, 0.9, 0.999)
    return {
        'x_prompt': nrm(0, (BATCH, SEQ, D_MODEL), 1.0),
        'x_sample': nrm(1, (DEC_BATCH, DEC_SEQ, D_MODEL), 1.0),
        'state_lru': nrm(2, (DEC_BATCH, DEPTH, 2, D_LRU), 0.5),
        'c': nrm(3, (DEC_BATCH, D_MODEL), 1.0),
        'c_ctx': nrm(4, (D_MODEL,), 1.0),
        'norm_mix': 1.0 + nrm(5, (DEPTH, D_MODEL), 0.05),
        'norm_ffn': 1.0 + nrm(6, (DEPTH, D_MODEL), 0.05),
        'norm_final': 1.0 + nrm(7, (D_MODEL,), 0.05),
        'ada_w': nrm(8, (DEPTH, D_MODEL, 6 * D_MODEL), 0.5 * D_MODEL ** -0.5),
        'ada_b': nrm(9, (DEPTH, 6 * D_MODEL), 0.02),
        'w_in': nrm(10, (DEPTH, D_MODEL, N_IN), D_MODEL ** -0.5),
        'w_out': nrm(11, (DEPTH, D_MIX, D_MODEL), D_MIX ** -0.5),
        'hyena_short_w': nrm(12, (DEPTH, HYENA_SHORT, 3 * D_HYENA), HYENA_SHORT ** -0.5),
        'hyena_short_b': nrm(13, (DEPTH, 3 * D_HYENA), 0.02),
        'filt_w1': nrm(14, (DEPTH, POS_DIM, FILTER_HIDDEN), 1.0),
        'filt_b1': nrm(15, (DEPTH, FILTER_HIDDEN), 0.1),
        'filt_w2': nrm(16, (DEPTH, FILTER_HIDDEN, FILTER_HIDDEN), FILTER_HIDDEN ** -0.5),
        'filt_b2': nrm(17, (DEPTH, FILTER_HIDDEN), 0.1),
        'filt_w3': nrm(18, (DEPTH, FILTER_HIDDEN, 2 * D_HYENA), FILTER_HIDDEN ** -0.5),
        'filt_b3': nrm(19, (DEPTH, 2 * D_HYENA), 0.1),
        'filt_freq': 1.0 + nrm(20, (DEPTH, 2, FILTER_HIDDEN), 0.1),
        'hyena_bias': nrm(21, (DEPTH, D_HYENA), 0.5),
        'lru_conv_w': nrm(22, (DEPTH, LRU_CONV, D_LRU), 0.5),
        'lru_conv_b': nrm(23, (DEPTH, D_LRU), 0.02),
        'lru_wa': nrm(24, (DEPTH, 2, N_LRU_HEADS, LRU_HEAD_DIM, LRU_HEAD_DIM), LRU_HEAD_DIM ** -0.5),
        'lru_ba': nrm(25, (DEPTH, 2, D_LRU), 0.1),
        'lru_wx': nrm(26, (DEPTH, 2, N_LRU_HEADS, LRU_HEAD_DIM, LRU_HEAD_DIM), LRU_HEAD_DIM ** -0.5),
        'lru_bx': nrm(27, (DEPTH, 2, D_LRU), 0.1),
        'lru_lambda': jnp.log(a0) - jnp.log1p(-a0),
        'ffn_w_gu': nrm(29, (N_DENSE, D_MODEL, 2 * D_FF), D_MODEL ** -0.5),
        'ffn_w_down': nrm(30, (N_DENSE, D_FF, D_MODEL), D_FF ** -0.5),
        'moe_router': nrm(31, (N_MOE, D_MODEL, N_EXPERTS), D_MODEL ** -0.5),
        'moe_w_gu': nrm(32, (N_MOE, N_EXPERTS, D_MODEL, 2 * D_FF), D_MODEL ** -0.5),
        'moe_w_down': nrm(33, (N_MOE, N_EXPERTS, D_FF, D_MODEL), D_FF ** -0.5),
    }


def reference(x_prompt, x_sample, state_lru, c, c_ctx, norm_mix, norm_ffn, norm_final, ada_w, ada_b,
              w_in, w_out, hyena_short_w, hyena_short_b, filt_w1, filt_b1, filt_w2, filt_b2, filt_w3,
              filt_b3, filt_freq, hyena_bias, lru_conv_w, lru_conv_b, lru_wa, lru_ba, lru_wx, lru_bx,
              lru_lambda, ffn_w_gu, ffn_w_down, moe_router, moe_w_gu, moe_w_down):
    xp = x_prompt
    xs = x_sample
    c_ctx_b = c_ctx[None, :]
    h0_ctx = jnp.zeros((x_prompt.shape[0], 2, D_LRU), jnp.float32)
    new_states = []
    for l in range(DEPTH):
        p = {
            'norm_mix': norm_mix[l], 'norm_ffn': norm_ffn[l], 'ada_w': ada_w[l], 'ada_b': ada_b[l],
            'w_in': w_in[l], 'w_out': w_out[l],
            'short_w': hyena_short_w[l], 'short_b': hyena_short_b[l],
            'filt_w1': filt_w1[l], 'filt_b1': filt_b1[l], 'filt_w2': filt_w2[l], 'filt_b2': filt_b2[l],
            'filt_w3': filt_w3[l], 'filt_b3': filt_b3[l], 'filt_freq': filt_freq[l], 'hyena_bias': hyena_bias[l],
            'lru_conv_w': lru_conv_w[l], 'lru_conv_b': lru_conv_b[l], 'lru_wa': lru_wa[l], 'lru_ba': lru_ba[l],
            'lru_wx': lru_wx[l], 'lru_bx': lru_bx[l], 'lru_lambda': lru_lambda[l],
        }
        m = l // 2
        if l % 2 == 0:
            ffn = functools.partial(_swiglu, w_gu=ffn_w_gu[m], w_down=ffn_w_down[m])
        else:
            ffn = functools.partial(_moe, router=moe_router[m], w_gu=moe_w_gu[m], w_down=moe_w_down[m])
        xp, st = _layer(xp, c_ctx_b, h0_ctx, False, p, ffn)
        new_states.append(st)
        xs, _ = _layer(xs, c, state_lru[:, l], True, p, ffn)
    new_state_lru = jnp.stack(new_states, axis=1).astype(state_lru.dtype)
    y_prompt = _rmsnorm(xp, norm_final)
    y_sample = _rmsnorm(xs, norm_final)
    return (y_prompt, y_sample, new_state_lru)
```

```python
import functools
import math

import numpy as np
import jax
import jax.numpy as jnp
from jax import lax
from jax.experimental import pallas as pl
from jax.experimental.pallas import tpu as pltpu

F32 = jnp.float32
BF16 = jnp.bfloat16

D_MODEL = 1024
DEPTH = 4
D_HYENA = 512
D_LRU = 512
N_IN = 3 * D_HYENA + 2 * D_LRU
N_LRU_HEADS = 8
LRU_HEAD_DIM = D_LRU // N_LRU_HEADS
LRU_C = 8.0
FILTER_HIDDEN = 64
N_BANDS = 4
POS_DIM = 1 + 2 * N_BANDS + 3
GRID_W = 64
HYENA_DECAY_FAST = 0.3
HYENA_DECAY_SLOW = 1.5
HYENA_DECAY_TARGET = 1e-2
HYENA_SHIFT = 0.05
D_FF = 2816
N_EXPERTS = 8
EPS = 1e-6

SUBLANES = 8
LANES = 128
VMEM_LIMIT = 56 * 1024 * 1024

FFT_N1 = 128
FFT_N2 = 64
K1_BLOCK = 8
N2_BLOCK = 8


def _cparams(sem):
    return pltpu.CompilerParams(dimension_semantics=sem, vmem_limit_bytes=VMEM_LIMIT)


def _split(a):
    hi = a.astype(BF16)
    lo = (a - hi.astype(F32)).astype(BF16)
    return hi, lo


def _dot(a, b, x3=False):
    if not x3:
        return jnp.dot(a.astype(BF16), b.astype(BF16), preferred_element_type=F32)
    ah, al = _split(a)
    bh, bl = _split(b)
    return (jnp.dot(ah, bh, preferred_element_type=F32)
            + jnp.dot(ah, bl, preferred_element_type=F32)
            + jnp.dot(al, bh, preferred_element_type=F32))


def _silu(x):
    return x * jax.nn.sigmoid(x)


def _gelu_tanh(x):
    return 0.5 * x * (1.0 + jnp.tanh(math.sqrt(2.0 / math.pi) * (x + 0.044715 * (x * x * x))))


def _rms(x, g):
    return x * lax.rsqrt(jnp.mean(x * x, axis=-1, keepdims=True) + EPS) * g


def _modulate(y, shift, scale):
    rows, d = y.shape
    y3 = y.reshape(rows // SUBLANES, SUBLANES, d)
    return (y3 * (1.0 + scale)[None] + shift[None]).reshape(rows, d)


@functools.lru_cache(maxsize=None)
def _dft_consts():
    n = FFT_N1 * FFT_N2
    half1 = FFT_N1 // 2
    k1 = np.arange(FFT_N1)[:, None]
    n1 = np.arange(half1)[None, :]
    ang = 2.0 * np.pi * k1 * n1 / FFT_N1
    c, s = np.cos(ang), np.sin(ang)
    m1 = np.block([[c, s], [-s, c]])
    k2 = np.arange(FFT_N2)[None, :, None]
    n2 = np.arange(FFT_N2)[None, None, :]
    kk1 = np.arange(FFT_N1)[:, None, None]
    th = 2.0 * np.pi * (n2 * k2 / FFT_N2 + n2 * kk1 / n)
    fr, fi = np.cos(th), -np.sin(th)
    g2 = np.concatenate([np.concatenate([fr, -fi], axis=2), np.concatenate([fi, fr], axis=2)], axis=1)
    g2t = np.transpose(g2, (0, 2, 1))
    cm, sm = c.T / n, s.T / n
    m3 = np.block([[cm, -sm], [sm, cm]])
    f32 = lambda a: np.asarray(a, np.float32)
    return f32(m1), f32(g2), f32(g2t), f32(m3)


@functools.lru_cache(maxsize=None)
def _dft_consts_direct(length):
    n = 2 * length
    f = np.arange(n)[:, None]
    t = np.arange(length)[None, :]
    ang = 2.0 * np.pi * f * t / n
    c, s = np.cos(ang), np.sin(ang)
    fwd = np.block([[c, s], [-s, c]])
    ct, st = c.T / n, s.T / n
    inv = np.block([[ct, -st], [st, ct]])
    f32 = lambda a: np.asarray(a, np.float32)
    return f32(fwd), f32(inv), f32(c), f32(-s)


def _hyena_deltas():
    min_decay = math.log(HYENA_DECAY_TARGET) / HYENA_DECAY_SLOW
    max_decay = math.log(HYENA_DECAY_TARGET) / HYENA_DECAY_FAST
    return jnp.abs(jnp.linspace(min_decay, max_decay, D_HYENA, dtype=F32))


def _pos_feats(length, grid):
    d = jnp.arange(length, dtype=jnp.int32)
    t = d.astype(F32) / length
    bands = jnp.arange(1, N_BANDS + 1, dtype=F32)
    ang = 2.0 * math.pi * t[:, None] * bands[None, :]
    feats = [t[:, None], jnp.sin(ang), jnp.cos(ang)]
    if grid:
        rows = max(length // GRID_W, 1)
        col_ang = 2.0 * math.pi * (d % GRID_W).astype(F32) / GRID_W
        row = (d // GRID_W).astype(F32) / rows
        feats += [jnp.sin(col_ang)[:, None], jnp.cos(col_ang)[:, None], row[:, None]]
    else:
        feats.append(jnp.zeros((length, 3), F32))
    z = jnp.concatenate(feats, axis=-1)
    return jnp.pad(z, ((0, 0), (0, LANES - POS_DIM)))


def _ada_kernel(cv_ref, w_ref, b_ref, o_ref):
    s = _silu(cv_ref[...])
    o_ref[...] = _dot(s, w_ref[...]) + b_ref[...]


def _ada_mods(cv, ada_w, ada_b):
    tn = D_MODEL
    return pl.pallas_call(
        _ada_kernel,
        out_shape=jax.ShapeDtypeStruct((DEPTH, 16, 6 * D_MODEL), F32),
        grid=(DEPTH, 6 * D_MODEL // tn),
        in_specs=[
            pl.BlockSpec((16, D_MODEL), lambda l, j: (0, 0)),
            pl.BlockSpec((None, D_MODEL, tn), lambda l, j: (l, 0, j)),
            pl.BlockSpec((None, 1, tn), lambda l, j: (l, 0, j)),
        ],
        out_specs=pl.BlockSpec((None, 16, tn), lambda l, j: (l, 0, j)),
        compiler_params=_cparams(("arbitrary", "arbitrary")),
        name="ada_mods",
    )(cv, ada_w, ada_b.reshape(DEPTH, 1, 6 * D_MODEL))


def _in_kernel(x_ref, g_ref, mod_ref, w_ref, o_ref):
    y = _rms(x_ref[...], g_ref[...])
    h = _modulate(y, mod_ref[:, 0:D_MODEL], mod_ref[:, D_MODEL:2 * D_MODEL])
    o_ref[...] = jnp.dot(h.astype(BF16), w_ref[...], preferred_element_type=F32)


def _in_proj(x, norm_mix, mods, w_in_bf, l, seg, tm=512):
    rows = x.shape[0]
    return pl.pallas_call(
        _in_kernel,
        out_shape=jax.ShapeDtypeStruct((rows, N_IN), F32),
        grid=(rows // tm,),
        in_specs=[
            pl.BlockSpec((tm, D_MODEL), lambda i: (i, 0)),
            pl.BlockSpec((None, 1, D_MODEL), lambda i: (l, 0, 0)),
            pl.BlockSpec((None, SUBLANES, 2 * D_MODEL), lambda i: (l, seg, 0)),
            pl.BlockSpec((None, D_MODEL, N_IN), lambda i: (l, 0, 0)),
        ],
        out_specs=pl.BlockSpec((tm, N_IN), lambda i: (i, 0)),
        compiler_params=_cparams(("arbitrary",)),
        name="in_proj",
    )(x, norm_mix.reshape(DEPTH, 1, D_MODEL), mods, w_in_bf)


def _conv_kernel(main_ref, prev_ref, next_ref, sw_ref, sb_ref, cw_ref, cb_ref,
                 u_ref, x2_ref, xr_ref, *, batch, tm):
    i = pl.program_id(0)
    last = pl.num_programs(0) - 1
    keep_prev = (i > 0).astype(F32)
    keep_next = (i < last).astype(F32)

    def tap(d, c0, c1):
        if d == 0:
            return main_ref[:, c0:c1]
        if d < 0:
            k = -d * batch
            head = prev_ref[2 * batch - k:2 * batch, c0:c1] * keep_prev
            return jnp.concatenate([head, main_ref[0:tm - k, c0:c1]], axis=0)
        k = d * batch
        tail = next_ref[0:k, c0:c1] * keep_next
        return jnp.concatenate([main_ref[k:tm, c0:c1], tail], axis=0)

    def hyena_branch(j):
        c0, c1 = j * D_HYENA, (j + 1) * D_HYENA
        acc = sb_ref[:, c0:c1] + tap(-1, c0, c1) * sw_ref[0:1, c0:c1]
        acc = acc + tap(0, c0, c1) * sw_ref[1:2, c0:c1]
        return acc + tap(1, c0, c1) * sw_ref[2:3, c0:c1]

    u_ref[...] = hyena_branch(0) * hyena_branch(1)
    x2_ref[...] = hyena_branch(2)
    c0, c1 = 3 * D_HYENA, 3 * D_HYENA + D_LRU
    acc = cb_ref[...] + tap(-2, c0, c1) * cw_ref[0:1, :]
    acc = acc + tap(-1, c0, c1) * cw_ref[1:2, :]
    acc = acc + tap(0, c0, c1) * cw_ref[2:3, :]
    xr_ref[...] = acc + tap(1, c0, c1) * cw_ref[3:4, :]


def _short_convs(proj, short_w, short_b, conv_w, conv_b, l, batch, tm=512):
    rows = proj.shape[0]
    width = 3 * D_HYENA + D_LRU
    per_prev = tm // (2 * batch)
    per_next = tm // batch
    n_next = rows // batch
    out = jax.ShapeDtypeStruct((rows, D_HYENA), F32)
    return pl.pallas_call(
        functools.partial(_conv_kernel, batch=batch, tm=tm),
        out_shape=(out, out, out),
        grid=(rows // tm,),
        in_specs=[
            pl.BlockSpec((tm, width), lambda i: (i, 0)),
            pl.BlockSpec((2 * batch, width), lambda i: (jnp.maximum(i * per_prev - 1, 0), 0)),
            pl.BlockSpec((batch, width), lambda i: (jnp.minimum((i + 1) * per_next, n_next - 1), 0)),
            pl.BlockSpec((None, 3, 3 * D_HYENA), lambda i: (l, 0, 0)),
            pl.BlockSpec((None, 1, 3 * D_HYENA), lambda i: (l, 0, 0)),
            pl.BlockSpec((None, 4, D_LRU), lambda i: (l, 0, 0)),
            pl.BlockSpec((None, 1, D_LRU), lambda i: (l, 0, 0)),
        ],
        out_specs=(pl.BlockSpec((tm, D_HYENA), lambda i: (i, 0)),) * 3,
        compiler_params=_cparams(("arbitrary",)),
        name="short_convs",
    )(proj, proj, proj, short_w, short_b.reshape(DEPTH, 1, 3 * D_HYENA),
      conv_w, conv_b.reshape(DEPTH, 1, D_LRU))


def _lru_gates(xr, w_ref, ba, bx, sp):
    xb = xr.astype(BF16)
    half = D_LRU // 2
    ra, rx = [], []
    for hf in range(2):
        xs = xb[:, hf * half:(hf + 1) * half]
        ra.append(jnp.dot(xs, w_ref[0, hf], preferred_element_type=F32))
        rx.append(jnp.dot(xs, w_ref[1, hf], preferred_element_type=F32))
    gate_a = jax.nn.sigmoid(jnp.concatenate(ra, axis=1) + ba)
    gate_x = jax.nn.sigmoid(jnp.concatenate(rx, axis=1) + bx)
    log_a = (-LRU_C) * gate_a * sp
    a = jnp.exp(log_a)
    mult = jnp.sqrt(jnp.maximum(jnp.tanh(-log_a) * (1.0 + a * a), 0.0))
    return a, xr * gate_x * mult


def _softplus(x):
    return jnp.maximum(x, 0.0) + jnp.log1p(jnp.exp(-jnp.abs(x)))


def _lru_kernel(*refs, batch, tm, chunk, reverse):
    if reverse:
        (xr_ref, w_ref, ba_ref, bx_ref, lam_ref, h0_ref, hf_ref, pg_ref,
         y_ref, hl_ref, a_scr, b_scr, h_scr) = refs
    else:
        (xr_ref, w_ref, ba_ref, bx_ref, lam_ref, h0_ref,
         y_ref, hl_ref, a_scr, b_scr, h_scr) = refs
    i = pl.program_id(0)

    @pl.when(i == 0)
    def _():
        h_scr[...] = h0_ref[...].astype(F32)

    sp = _softplus(-lam_ref[...])
    ba = ba_ref[...]
    bx = bx_ref[...]

    def fill(c, carry):
        r = pl.multiple_of(c * chunk, chunk)
        a, b = _lru_gates(xr_ref[pl.ds(r, chunk), :], w_ref, ba, bx, sp)
        a_scr[pl.ds(r, chunk), :] = a
        b_scr[pl.ds(r, chunk), :] = b
        return carry

    lax.fori_loop(0, tm // chunk, fill, 0)

    steps = tm // batch

    def step(s, h):
        t = (steps - 1 - s) if reverse else s
        r = pl.multiple_of(t * batch, batch)
        h = a_scr[pl.ds(r, batch), :] * h + b_scr[pl.ds(r, batch), :]
        if reverse:
            y_ref[pl.ds(r, batch), :] = ((hf_ref[pl.ds(r, batch), :] + h)
                                         * _gelu_tanh(pg_ref[pl.ds(r, batch), :]))
        else:
            y_ref[pl.ds(r, batch), :] = h
        return h

    h = lax.fori_loop(0, steps, step, h_scr[...], unroll=8)
    h_scr[...] = h

    @pl.when(i == pl.num_programs(0) - 1)
    def _():
        hl_ref[...] = h


def _lru_scan(xr, wbd, lru_ba, lru_bx, lru_lambda, h0, h0_block, l, direction, batch,
              hf=None, proj=None, tm=2048, chunk=256):
    rows = xr.shape[0]
    nt = rows // tm
    reverse = direction == 1
    row_map = (lambda i: (nt - 1 - i, 0)) if reverse else (lambda i: (i, 0))
    vec = lambda: pl.BlockSpec((None, None, 1, D_LRU), lambda i: (l, direction, 0, 0))
    in_specs = [
        pl.BlockSpec((tm, D_LRU), row_map),
        pl.BlockSpec((None, None, 2, 2, D_LRU // 2, D_LRU // 2), lambda i: (l, direction, 0, 0, 0, 0)),
        vec(), vec(), vec(),
        pl.BlockSpec((batch, D_LRU), lambda i: h0_block),
    ]
    args = [xr, wbd, lru_ba.reshape(DEPTH, 2, 1, D_LRU), lru_bx.reshape(DEPTH, 2, 1, D_LRU),
            lru_lambda.reshape(DEPTH, 2, 1, D_LRU), h0]
    if reverse:
        in_specs += [pl.BlockSpec((tm, D_LRU), row_map),
                     pl.BlockSpec((tm, D_LRU), lambda i: (nt - 1 - i, (3 * D_HYENA + D_LRU) // D_LRU))]
        args += [hf, proj]
    return pl.pallas_call(
        functools.partial(_lru_kernel, batch=batch, tm=tm, chunk=chunk, reverse=reverse),
        out_shape=(jax.ShapeDtypeStruct((rows, D_LRU), F32), jax.ShapeDtypeStruct((batch, D_LRU), F32)),
        grid=(nt,),
        in_specs=in_specs,
        out_specs=(pl.BlockSpec((tm, D_LRU), row_map), pl.BlockSpec((batch, D_LRU), lambda i: (0, 0))),
        scratch_shapes=[pltpu.VMEM((tm, D_LRU), F32), pltpu.VMEM((tm, D_LRU), F32),
                        pltpu.VMEM((batch, D_LRU), F32)],
        compiler_params=_cparams(("arbitrary",)),
        name="lru_rev" if reverse else "lru_fwd",
    )(*args)


def _filter_kernel(z_ref, w1_ref, b1_ref, w2_ref, b2_ref, w3_ref, b3_ref, fr_ref, dl_ref,
                   ks_ref, kd_ref, sum_ref):
    i = pl.program_id(0)
    hi = lax.Precision.HIGHEST
    z = z_ref[...]
    h1 = jnp.sin(fr_ref[0:1, :] * (jnp.dot(z, w1_ref[...], precision=hi, preferred_element_type=F32)
                                   + b1_ref[...]))
    h2 = jnp.sin(fr_ref[1:2, :] * (jnp.dot(h1, w2_ref[...], precision=hi, preferred_element_type=F32)
                                   + b2_ref[...]))
    k = jnp.dot(h2, w3_ref[...], precision=hi, preferred_element_type=F32) + b3_ref[...]
    window = jnp.exp(-z[:, 0:1] * dl_ref[...]) + HYENA_SHIFT
    kf = k[:, :D_HYENA] * window
    kb = k[:, D_HYENA:] * window
    row = lax.broadcasted_iota(jnp.int32, kb.shape, 0) + i * kb.shape[0]
    kb = jnp.where(row == 0, 0.0, kb)
    ks_ref[...] = kf + kb
    kd_ref[...] = kf - kb
    part = jnp.sum(jnp.abs(kf) + jnp.abs(kb), axis=0, keepdims=True)

    @pl.when(i == 0)
    def _():
        sum_ref[...] = part

    @pl.when(i > 0)
    def _():
        sum_ref[...] += part


def _hyena_filter_taps(length, grid, p, l, tl=256):
    z = _pos_feats(length, grid)
    w1 = jnp.pad(p['filt_w1'], ((0, 0), (0, LANES - POS_DIM), (0, 0)))
    tap = jax.ShapeDtypeStruct((length, D_HYENA), F32)
    h = FILTER_HIDDEN
    return pl.pallas_call(
        _filter_kernel,
        out_shape=(tap, tap, jax.ShapeDtypeStruct((1, D_HYENA), F32)),
        grid=(length // tl,),
        in_specs=[
            pl.BlockSpec((tl, LANES), lambda i: (i, 0)),
            pl.BlockSpec((None, LANES, h), lambda i: (l, 0, 0)),
            pl.BlockSpec((None, 1, h), lambda i: (l, 0, 0)),
            pl.BlockSpec((None, h, h), lambda i: (l, 0, 0)),
            pl.BlockSpec((None, 1, h), lambda i: (l, 0, 0)),
            pl.BlockSpec((None, h, 2 * D_HYENA), lambda i: (l, 0, 0)),
            pl.BlockSpec((None, 1, 2 * D_HYENA), lambda i: (l, 0, 0)),
            pl.BlockSpec((None, 2, h), lambda i: (l, 0, 0)),
            pl.BlockSpec((1, D_HYENA), lambda i: (0, 0)),
        ],
        out_specs=(pl.BlockSpec((tl, D_HYENA), lambda i: (i, 0)),
                   pl.BlockSpec((tl, D_HYENA), lambda i: (i, 0)),
                   pl.BlockSpec((1, D_HYENA), lambda i: (0, 0))),
        compiler_params=_cparams(("arbitrary",)),
        name="hyena_filter",
    )(z, w1, p['filt_b1'].reshape(DEPTH, 1, h), p['filt_w2'], p['filt_b2'].reshape(DEPTH, 1, h),
      p['filt_w3'], p['filt_b3'].reshape(DEPTH, 1, 2 * D_HYENA), p['filt_freq'],
      _hyena_deltas().reshape(1, D_HYENA))


def _spec_direct_kernel(c_ref, s_ref, ks_ref, kd_ref, sum_ref, o_ref):
    scale = 1.0 / (sum_ref[...] + EPS)
    o_ref[0] = _dot(c_ref[...], ks_ref[...], x3=True) * scale
    o_ref[1] = _dot(s_ref[...], kd_ref[...], x3=True) * scale


def _filter_spectrum_direct(ks, kd, ksum):
    length = ks.shape[0]
    n = 2 * length
    _, _, c, ms = _dft_consts_direct(length)
    return pl.pallas_call(
        _spec_direct_kernel,
        out_shape=jax.ShapeDtypeStruct((2, n, D_HYENA), F32),
        compiler_params=_cparams(()),
        name="filter_spectrum_direct",
    )(jnp.asarray(c), jnp.asarray(ms), ks, kd, ksum)


def _conv_direct_kernel(u_ref, x2_ref, kh_ref, fwd_ref, inv_ref, bias_ref, o_ref, *, length, batch):
    n = 2 * length
    kr, ki = kh_ref[0], kh_ref[1]
    for q in range(batch // 2):
        z = jnp.concatenate([u_ref[:, 2 * q, :], u_ref[:, 2 * q + 1, :]], axis=0)
        x = _dot(fwd_ref[...], z, x3=True)
        xr, xi = x[:n], x[n:]
        y = jnp.concatenate([xr * kr - xi * ki, xr * ki + xi * kr], axis=0)
        conv = _dot(inv_ref[...], y, x3=True)
        o_ref[:, 2 * q, :] = conv[:length]
        o_ref[:, 2 * q + 1, :] = conv[length:]
    o_ref[...] = x2_ref[...] * (o_ref[...] + u_ref[...] * bias_ref[...])


def _hyena_conv_direct(u, x2, khat, hyena_bias, l, batch, cw=256):
    rows = u.shape[0]
    length = rows // batch
    fwd, inv, _, _ = _dft_consts_direct(length)
    blk = pl.BlockSpec((length, batch, cw), lambda j: (0, 0, j))
    out = pl.pallas_call(
        functools.partial(_conv_direct_kernel, length=length, batch=batch),
        out_shape=jax.ShapeDtypeStruct((length, batch, D_HYENA), F32),
        grid=(D_HYENA // cw,),
        in_specs=[blk, blk,
                  pl.BlockSpec((2, 2 * length, cw), lambda j: (0, 0, j)),
                  pl.BlockSpec(fwd.shape, lambda j: (0, 0)),
                  pl.BlockSpec(inv.shape, lambda j: (0, 0)),
                  pl.BlockSpec((None, 1, cw), lambda j: (l, 0, j))],
        out_specs=blk,
        compiler_params=_cparams(("arbitrary",)),
        name="hyena_conv_direct",
    )(u.reshape(length, batch, D_HYENA), x2.reshape(length, batch, D_HYENA), khat,
      jnp.asarray(fwd), jnp.asarray(inv), hyena_bias.reshape(DEPTH, 1, D_HYENA))
    return out.reshape(rows, D_HYENA)


def _stage1_kernel(x_ref, m_ref, o_ref, *, batch):
    m = m_ref[...].astype(BF16)
    for j in range(N2_BLOCK):
        for q in range(batch // 2):
            z = jnp.concatenate([x_ref[:, j, 2 * q, :], x_ref[:, j, 2 * q + 1, :]], axis=0)
            r = jnp.dot(m, z.astype(BF16), preferred_element_type=F32)
            o_ref[0, :, q, j, :] = r[:FFT_N1]
            o_ref[1, :, q, j, :] = r[FFT_N1:]


def _fft_stage1(u, batch, cw=256):
    c = u.shape[1]
    half1 = FFT_N1 // 2
    m1, _, _, _ = _dft_consts()
    return pl.pallas_call(
        functools.partial(_stage1_kernel, batch=batch),
        out_shape=jax.ShapeDtypeStruct((2, FFT_N1, batch // 2, FFT_N2, c), F32),
        grid=(FFT_N2 // N2_BLOCK, c // cw),
        in_specs=[pl.BlockSpec((half1, N2_BLOCK, batch, cw), lambda j, k: (0, j, 0, k)),
                  pl.BlockSpec(m1.shape, lambda j, k: (0, 0))],
        out_specs=pl.BlockSpec((2, FFT_N1, batch // 2, N2_BLOCK, cw), lambda j, k: (0, 0, 0, j, k)),
        compiler_params=_cparams(("arbitrary", "arbitrary")),
        name="fft_stage1",
    )(u.reshape(half1, FFT_N2, batch, c), jnp.asarray(m1))


def _stage1_filter_kernel(ks_ref, kd_ref, m_ref, o_ref):
    m = m_ref[:, :FFT_N1 // 2]
    for j in range(N2_BLOCK):
        for g, ref in enumerate((ks_ref, kd_ref)):
            r = _dot(m, ref[:, j, :], x3=True)
            o_ref[0, :, g, j, :] = r[:FFT_N1]
            o_ref[1, :, g, j, :] = r[FFT_N1:]


def _stage2_filter_kernel(a_ref, g_ref, sum_ref, o_ref):
    scale = 1.0 / (sum_ref[...] + EPS)
    for j in range(K1_BLOCK):
        bs = jnp.concatenate([a_ref[0, j, 0], a_ref[1, j, 0]], axis=0)
        bd = jnp.concatenate([a_ref[0, j, 1], a_ref[1, j, 1]], axis=0)
        o_ref[0, j] = _dot(g_ref[j], bs, x3=True)[:FFT_N2] * scale
        o_ref[1, j] = _dot(g_ref[j], bd, x3=True)[FFT_N2:] * scale


def _filter_spectrum_2stage(ks, kd, ksum):
    c = D_HYENA
    half1 = FFT_N1 // 2
    m1, g2, _, _ = _dft_consts()
    tap = pl.BlockSpec((half1, N2_BLOCK, c), lambda j: (0, j, 0))
    a = pl.pallas_call(
        _stage1_filter_kernel,
        out_shape=jax.ShapeDtypeStruct((2, FFT_N1, 2, FFT_N2, c), F32),
        grid=(FFT_N2 // N2_BLOCK,),
        in_specs=[tap, tap, pl.BlockSpec(m1.shape, lambda j: (0, 0))],
        out_specs=pl.BlockSpec((2, FFT_N1, 2, N2_BLOCK, c), lambda j: (0, 0, 0, j, 0)),
        compiler_params=_cparams(("arbitrary",)),
        name="filter_spectrum_stage1",
    )(ks.reshape(half1, FFT_N2, c), kd.reshape(half1, FFT_N2, c), jnp.asarray(m1))
    return pl.pallas_call(
        _stage2_filter_kernel,
        out_shape=jax.ShapeDtypeStruct((2, FFT_N1, FFT_N2, c), F32),
        grid=(FFT_N1 // K1_BLOCK,),
        in_specs=[pl.BlockSpec((2, K1_BLOCK, 2, FFT_N2, c), lambda i: (0, i, 0, 0, 0)),
                  pl.BlockSpec((K1_BLOCK, 2 * FFT_N2, 2 * FFT_N2), lambda i: (i, 0, 0)),
                  pl.BlockSpec((1, c), lambda i: (0, 0))],
        out_specs=pl.BlockSpec((2, K1_BLOCK, FFT_N2, c), lambda i: (0, i, 0, 0)),
        compiler_params=_cparams(("arbitrary",)),
        name="filter_spectrum_stage2",
    )(a, jnp.asarray(g2), ksum)


def _stage2_kernel(a_ref, kh_ref, g_ref, gt_ref, o_ref):
    for j in range(K1_BLOCK):
        b = jnp.concatenate([a_ref[0, j], a_ref[1, j]], axis=0)
        x = _dot(g_ref[j], b)
        xr, xi = x[:FFT_N2], x[FFT_N2:]
        kr, ki = kh_ref[0, j], kh_ref[1, j]
        y = jnp.concatenate([xr * kr - xi * ki, xr * ki + xi * kr], axis=0)
        r = _dot(gt_ref[j], y)
        o_ref[0, j] = r[:FFT_N2]
        o_ref[1, j] = r[FFT_N2:]


def _fft_stage2(a, khat):
    _, g2, g2t, _ = _dft_consts()
    c = D_HYENA
    pairs = a.shape[2]
    ablk = pl.BlockSpec((2, K1_BLOCK, None, FFT_N2, c), lambda i, q: (0, i, q, 0, 0))
    gblk = pl.BlockSpec((K1_BLOCK, 2 * FFT_N2, 2 * FFT_N2), lambda i, q: (i, 0, 0))
    return pl.pallas_call(
        _stage2_kernel,
        out_shape=jax.ShapeDtypeStruct(a.shape, F32),
        grid=(FFT_N1 // K1_BLOCK, pairs),
        in_specs=[ablk, pl.BlockSpec((2, K1_BLOCK, FFT_N2, c), lambda i, q: (0, i, 0, 0)), gblk, gblk],
        out_specs=ablk,
        compiler_params=_cparams(("arbitrary", "arbitrary")),
        name="fft_stage2",
    )(a, khat, jnp.asarray(g2), jnp.asarray(g2t))


def _stage3_kernel(a_ref, m_ref, u_ref, x2_ref, bias_ref, o_ref, *, batch):
    half1 = FFT_N1 // 2
    m = m_ref[...].astype(BF16)
    for j in range(N2_BLOCK):
        for q in range(batch // 2):
            b = jnp.concatenate([a_ref[0, :, q, j, :], a_ref[1, :, q, j, :]], axis=0)
            y = jnp.dot(m, b.astype(BF16), preferred_element_type=F32)
            o_ref[:, j, 2 * q, :] = y[:half1]
            o_ref[:, j, 2 * q + 1, :] = y[half1:]
    o_ref[...] = x2_ref[...] * (o_ref[...] + u_ref[...] * bias_ref[...])


def _fft_stage3(a, u, x2, hyena_bias, l, batch, cw=256):
    rows, c = u.shape
    half1 = FFT_N1 // 2
    _, _, _, m3 = _dft_consts()
    blk = pl.BlockSpec((half1, N2_BLOCK, batch, cw), lambda j, k: (0, j, 0, k))
    out = pl.pallas_call(
        functools.partial(_stage3_kernel, batch=batch),
        out_shape=jax.ShapeDtypeStruct((half1, FFT_N2, batch, c), F32),
        grid=(FFT_N2 // N2_BLOCK, c // cw),
        in_specs=[pl.BlockSpec((2, FFT_N1, batch // 2, N2_BLOCK, cw), lambda j, k: (0, 0, 0, j, k)),
                  pl.BlockSpec(m3.shape, lambda j, k: (0, 0)),
                  blk, blk,
                  pl.BlockSpec((None, 1, cw), lambda j, k: (l, 0, k))],
        out_specs=blk,
        compiler_params=_cparams(("arbitrary", "arbitrary")),
        name="fft_stage3",
    )(a, jnp.asarray(m3), u.reshape(half1, FFT_N2, batch, c), x2.reshape(half1, FFT_N2, batch, c),
      hyena_bias.reshape(DEPTH, 1, c))
    return out.reshape(rows, c)


def _hyena_conv_2stage(u, x2, khat, hyena_bias, l, batch):
    a = _fft_stage1(u, batch)
    a = _fft_stage2(a, khat)
    return _fft_stage3(a, u, x2, hyena_bias, l, batch)


def _out_kernel(hy_ref, lr_ref, x_ref, w_ref, g_ref, mod_ref, xo_ref, h_ref):
    d = D_MODEL
    mixed = (jnp.dot(hy_ref[...].astype(BF16), w_ref[0:D_HYENA, :], preferred_element_type=F32)
             + jnp.dot(lr_ref[...].astype(BF16), w_ref[D_HYENA:, :], preferred_element_type=F32))
    rows = mixed.shape[0]
    g1 = mod_ref[:, 2 * d:3 * d]
    x = x_ref[...] + (mixed.reshape(rows // SUBLANES, SUBLANES, d) * g1[None]).reshape(rows, d)
    xo_ref[...] = x
    h = _modulate(_rms(x, g_ref[...]), mod_ref[:, 3 * d:4 * d], mod_ref[:, 4 * d:5 * d])
    h_ref[...] = h.astype(BF16)


def _out_proj(hy, lr, x, w_out_bf, norm_ffn, mods, l, seg, tm=512):
    rows = x.shape[0]
    d = D_MODEL
    return pl.pallas_call(
        _out_kernel,
        out_shape=(jax.ShapeDtypeStruct((rows, d), F32), jax.ShapeDtypeStruct((rows, d), BF16)),
        grid=(rows // tm,),
        in_specs=[
            pl.BlockSpec((tm, D_HYENA), lambda i: (i, 0)),
            pl.BlockSpec((tm, D_LRU), lambda i: (i, 0)),
            pl.BlockSpec((tm, d), lambda i: (i, 0)),
            pl.BlockSpec((None, D_HYENA + D_LRU, d), lambda i: (l, 0, 0)),
            pl.BlockSpec((None, 1, d), lambda i: (l, 0, 0)),
            pl.BlockSpec((None, SUBLANES, 6 * d), lambda i: (l, seg, 0)),
        ],
        out_specs=(pl.BlockSpec((tm, d), lambda i: (i, 0)), pl.BlockSpec((tm, d), lambda i: (i, 0))),
        compiler_params=_cparams(("arbitrary",)),
        name="out_proj",
    )(hy, lr, x, w_out_bf, norm_ffn.reshape(DEPTH, 1, d), mods)


def _gate_residual(x, acc, g2):
    rows, d = acc.shape
    return x + (acc.reshape(rows // SUBLANES, SUBLANES, d) * g2[None]).reshape(rows, d)


def _ffn_kernel(h_ref, x_ref, wg_ref, wu_ref, wd_ref, mod_ref, o_ref, acc_ref):
    f = pl.program_id(1)
    h = h_ref[...]
    g = jnp.dot(h, wg_ref[...], preferred_element_type=F32)
    u = jnp.dot(h, wu_ref[...], preferred_element_type=F32)
    part = jnp.dot((_silu(g) * u).astype(BF16), wd_ref[...], preferred_element_type=F32)

    @pl.when(f == 0)
    def _():
        acc_ref[...] = part

    @pl.when(f > 0)
    def _():
        acc_ref[...] += part

    @pl.when(f == pl.num_programs(1) - 1)
    def _():
        o_ref[...] = _gate_residual(x_ref[...], acc_ref[...], mod_ref[...])


def _ffn_dense(h, x, w_gu_bf, w_down_bf, mods, m, l, seg, tm=512, fc=1408):
    rows = x.shape[0]
    d = D_MODEL
    nf = D_FF // fc
    return pl.pallas_call(
        _ffn_kernel,
        out_shape=jax.ShapeDtypeStruct((rows, d), F32),
        grid=(rows // tm, nf),
        in_specs=[
            pl.BlockSpec((tm, d), lambda i, f: (i, 0)),
            pl.BlockSpec((tm, d), lambda i, f: (i, 0)),
            pl.BlockSpec((None, d, fc), lambda i, f: (m, 0, f)),
            pl.BlockSpec((None, d, fc), lambda i, f: (m, 0, nf + f)),
            pl.BlockSpec((None, fc, d), lambda i, f: (m, f, 0)),
            pl.BlockSpec((None, SUBLANES, d), lambda i, f: (l, seg, 5)),
        ],
        out_specs=pl.BlockSpec((tm, d), lambda i, f: (i, 0)),
        scratch_shapes=[pltpu.VMEM((tm, d), F32)],
        compiler_params=_cparams(("arbitrary", "arbitrary")),
        name="ffn_dense",
    )(h, x, w_gu_bf, w_gu_bf, w_down_bf, mods)


def _route_top2(logits):
    lane = lax.broadcasted_iota(jnp.int32, logits.shape, 1)
    neg = jnp.float32(-jnp.inf)
    v = jnp.where(lane < N_EXPERTS, logits, neg)
    m1 = jnp.max(v, axis=-1, keepdims=True)
    i1 = jnp.min(jnp.where(v == m1, lane, LANES), axis=-1, keepdims=True)
    v2 = jnp.where(lane == i1, neg, v)
    m2 = jnp.max(v2, axis=-1, keepdims=True)
    i2 = jnp.min(jnp.where(v2 == m2, lane, LANES), axis=-1, keepdims=True)
    e2 = jnp.exp(m2 - m1)
    w1 = 1.0 / (1.0 + e2)
    w2 = e2 / (1.0 + e2)
    return jnp.where(lane == i1, w1, 0.0) + jnp.where(lane == i2, w2, 0.0)


def _moe_kernel(h_ref, x_ref, r_ref, wg_ref, wu_ref, wd_ref, mod_ref, o_ref, acc_ref, comb_ref):
    e = pl.program_id(1)
    f = pl.program_id(2)
    h = h_ref[...]

    @pl.when((e == 0) & (f == 0))
    def _():
        comb_ref[...] = _route_top2(_dot(h.astype(F32), r_ref[...], x3=True))
        acc_ref[...] = jnp.zeros_like(acc_ref)

    g = jnp.dot(h, wg_ref[...], preferred_element_type=F32)
    u = jnp.dot(h, wu_ref[...], preferred_element_type=F32)
    part = jnp.dot((_silu(g) * u).astype(BF16), wd_ref[...], preferred_element_type=F32)
    lane = lax.broadcasted_iota(jnp.int32, comb_ref.shape, 1)
    w_e = jnp.sum(jnp.where(lane == e, comb_ref[...], 0.0), axis=-1, keepdims=True)
    acc_ref[...] += w_e * part

    @pl.when((e == pl.num_programs(1) - 1) & (f == pl.num_programs(2) - 1))
    def _():
        o_ref[...] = _gate_residual(x_ref[...], acc_ref[...], mod_ref[...])


def _ffn_moe(h, x, router_pad, w_gu_bf, w_down_bf, mods, m, l, seg, tm=512, fc=1408):
    rows = x.shape[0]
    d = D_MODEL
    nf = D_FF // fc
    return pl.pallas_call(
        _moe_kernel,
        out_shape=jax.ShapeDtypeStruct((rows, d), F32),
        grid=(rows // tm, N_EXPERTS, nf),
        in_specs=[
            pl.BlockSpec((tm, d), lambda i, e, f: (i, 0)),
            pl.BlockSpec((tm, d), lambda i, e, f: (i, 0)),
            pl.BlockSpec((None, d, LANES), lambda i, e, f: (m, 0, 0)),
            pl.BlockSpec((None, None, d, fc), lambda i, e, f: (m, e, 0, f)),
            pl.BlockSpec((None, None, d, fc), lambda i, e, f: (m, e, 0, nf + f)),
            pl.BlockSpec((None, None, fc, d), lambda i, e, f: (m, e, f, 0)),
            pl.BlockSpec((None, SUBLANES, d), lambda i, e, f: (l, seg, 5)),
        ],
        out_specs=pl.BlockSpec((tm, d), lambda i, e, f: (i, 0)),
        scratch_shapes=[pltpu.VMEM((tm, d), F32), pltpu.VMEM((tm, LANES), F32)],
        compiler_params=_cparams(("arbitrary", "arbitrary", "arbitrary")),
        name="ffn_moe",
    )(h, x, router_pad, w_gu_bf, w_gu_bf, w_down_bf, mods)


def _final_kernel(x_ref, g_ref, o_ref, *, batch):
    for b in range(batch):
        o_ref[b] = _rms(x_ref[:, b, :], g_ref[...])


def _final_norm(x, norm_final, batch, tl=64):
    rows, d = x.shape
    length = rows // batch
    return pl.pallas_call(
        functools.partial(_final_kernel, batch=batch),
        out_shape=jax.ShapeDtypeStruct((batch, length, d), F32),
        grid=(length // tl,),
        in_specs=[pl.BlockSpec((tl, batch, d), lambda i: (i, 0, 0)),
                  pl.BlockSpec((1, d), lambda i: (0, 0))],
        out_specs=pl.BlockSpec((batch, tl, d), lambda i: (0, i, 0)),
        compiler_params=_cparams(("arbitrary",)),
        name="final_norm",
    )(x.reshape(length, batch, d), norm_final.reshape(1, d))


def _time_major_kernel(x_ref, o_ref, *, batch):
    for b in range(batch):
        o_ref[:, b, :] = x_ref[b]


def _to_time_major(x, tl=64):
    batch, length, d = x.shape
    out = pl.pallas_call(
        functools.partial(_time_major_kernel, batch=batch),
        out_shape=jax.ShapeDtypeStruct((length, batch, d), x.dtype),
        grid=(length // tl,),
        in_specs=[pl.BlockSpec((batch, tl, d), lambda i: (0, i, 0))],
        out_specs=pl.BlockSpec((tl, batch, d), lambda i: (i, 0, 0)),
        compiler_params=_cparams(("arbitrary",)),
        name="to_time_major",
    )(x)
    return out.reshape(length * batch, d)


def _block_diag_heads(w):
    hh = N_LRU_HEADS // 2
    w = w.reshape(DEPTH, 2, 2, hh, LRU_HEAD_DIM, LRU_HEAD_DIM)
    eye = jnp.eye(hh, dtype=w.dtype)
    bd = jnp.einsum('ldgpij,pq->ldgpiqj', w, eye)
    return bd.reshape(DEPTH, 2, 2, hh * LRU_HEAD_DIM, hh * LRU_HEAD_DIM)


def kernel(x_prompt, x_sample, state_lru, c, c_ctx, norm_mix, norm_ffn, norm_final, ada_w, ada_b,
           w_in, w_out, hyena_short_w, hyena_short_b, filt_w1, filt_b1, filt_w2, filt_b2, filt_w3,
           filt_b3, filt_freq, hyena_bias, lru_conv_w, lru_conv_b, lru_wa, lru_ba, lru_wx, lru_bx,
           lru_lambda, ffn_w_gu, ffn_w_down, moe_router, moe_w_gu, moe_w_down):
    bp, lp, d = x_prompt.shape
    bs, ls, _ = x_sample.shape
    assert bs == SUBLANES and bp % SUBLANES == 0 and ls == (FFT_N1 // 2) * FFT_N2

    xs = _to_time_major(x_sample)
    xp = _to_time_major(x_prompt)

    cv = jnp.concatenate([c, jnp.broadcast_to(c_ctx[None, :], (SUBLANES, d))], axis=0)
    mods = _ada_mods(cv, ada_w, ada_b)

    w_in_bf = w_in.astype(BF16)
    w_out_bf = w_out.astype(BF16)
    ffn_gu_bf = ffn_w_gu.astype(BF16)
    ffn_down_bf = ffn_w_down.astype(BF16)
    moe_gu_bf = moe_w_gu.astype(BF16)
    moe_down_bf = moe_w_down.astype(BF16)
    router_pad = jnp.pad(moe_router, ((0, 0), (0, 0), (0, LANES - N_EXPERTS)))
    wbd = jnp.stack([_block_diag_heads(lru_wa), _block_diag_heads(lru_wx)], axis=2).astype(BF16)
    filt = dict(filt_w1=filt_w1, filt_b1=filt_b1, filt_w2=filt_w2, filt_b2=filt_b2,
                filt_w3=filt_w3, filt_b3=filt_b3, filt_freq=filt_freq)
    state_flat = state_lru.reshape(bs, DEPTH * 2 * D_LRU)
    zero_state = jnp.zeros((bp, D_LRU), F32)

    new_states = []
    for l in range(DEPTH):
        m = l // 2
        for path in range(2):
            if path == 0:
                x, batch, seg, grid_pos = xp, bp, 1, False
            else:
                x, batch, seg, grid_pos = xs, bs, 0, True
            length = x.shape[0] // batch

            proj = _in_proj(x, norm_mix, mods, w_in_bf, l, seg)
            u, x2, xr = _short_convs(proj, hyena_short_w, hyena_short_b, lru_conv_w, lru_conv_b, l, batch)

            ks, kd, ksum = _hyena_filter_taps(length, grid_pos, filt, l)
            if path == 0:
                khat = _filter_spectrum_direct(ks, kd, ksum)
                hy = _hyena_conv_direct(u, x2, khat, hyena_bias, l, batch)
            else:
                khat = _filter_spectrum_2stage(ks, kd, ksum)
                hy = _hyena_conv_2stage(u, x2, khat, hyena_bias, l, batch)

            lru_tm = min(2048, x.shape[0])
            if path == 0:
                h0f, h0f_blk, h0r, h0r_blk = zero_state, (0, 0), zero_state, (0, 0)
            else:
                h0f, h0f_blk, h0r, h0r_blk = state_flat, (0, 2 * l), state_flat, (0, 2 * l + 1)
            hf, hlast_f = _lru_scan(xr, wbd, lru_ba, lru_bx, lru_lambda, h0f, h0f_blk, l, 0, batch,
                                    tm=lru_tm)
            lr, hlast_r = _lru_scan(xr, wbd, lru_ba, lru_bx, lru_lambda, h0r, h0r_blk, l, 1, batch,
                                    hf=hf, proj=proj, tm=lru_tm)
            if path == 0:
                new_states.append(jnp.stack([hlast_f, hlast_r], axis=1))

            x, h = _out_proj(hy, lr, x, w_out_bf, norm_ffn, mods, l, seg)
            if l % 2 == 0:
                x = _ffn_dense(h, x, ffn_gu_bf, ffn_down_bf, mods, m, l, seg)
            else:
                x = _ffn_moe(h, x, router_pad, moe_gu_bf, moe_down_bf, mods, m, l, seg)
            if path == 0:
                xp = x
            else:
                xs = x

    new_state_lru = jnp.stack(new_states, axis=1).astype(state_lru.dtype)
    y_prompt = _final_norm(xp, norm_final, bp)
    y_sample = _final_norm(xs, norm_final, bs)
    return (y_prompt, y_sample, new_state_lru)
```

```python
import functools
import math

import numpy as np
import jax
import jax.numpy as jnp
from jax import lax
from jax.experimental import pallas as pl
from jax.experimental.pallas import tpu as pltpu

F32 = jnp.float32
BF16 = jnp.bfloat16

D_MODEL = 1024
DEPTH = 4
D_HYENA = 512
D_LRU = 512
N_IN = 3 * D_HYENA + 2 * D_LRU
N_LRU_HEADS = 8
LRU_HEAD_DIM = D_LRU // N_LRU_HEADS
LRU_C = 8.0
FILTER_HIDDEN = 64
N_BANDS = 4
POS_DIM = 1 + 2 * N_BANDS + 3
GRID_W = 64
HYENA_DECAY_FAST = 0.3
HYENA_DECAY_SLOW = 1.5
HYENA_DECAY_TARGET = 1e-2
HYENA_SHIFT = 0.05
D_FF = 2816
N_EXPERTS = 8
EPS = 1e-6

SUBLANES = 8
LANES = 128
VMEM_LIMIT = 56 * 1024 * 1024

FFT_N1 = 128
FFT_N2 = 64
K1_BLOCK = 8
N2_BLOCK = 8


def _cparams(sem):
    return pltpu.CompilerParams(dimension_semantics=sem, vmem_limit_bytes=VMEM_LIMIT)


def _split(a):
    hi = a.astype(BF16)
    lo = (a - hi.astype(F32)).astype(BF16)
    return hi, lo


def _dot(a, b, x3=False):
    if not x3:
        return jnp.dot(a.astype(BF16), b.astype(BF16), preferred_element_type=F32)
    ah, al = _split(a)
    bh, bl = _split(b)
    return (jnp.dot(ah, bh, preferred_element_type=F32)
            + jnp.dot(ah, bl, preferred_element_type=F32)
            + jnp.dot(al, bh, preferred_element_type=F32))


def _silu(x):
    return x * jax.nn.sigmoid(x)


def _gelu_tanh(x):
    return 0.5 * x * (1.0 + jnp.tanh(math.sqrt(2.0 / math.pi) * (x + 0.044715 * (x * x * x))))


def _rms(x, g):
    return x * lax.rsqrt(jnp.mean(x * x, axis=-1, keepdims=True) + EPS) * g


def _modulate(y, shift, scale):
    rows, d = y.shape
    y3 = y.reshape(rows // SUBLANES, SUBLANES, d)
    return (y3 * (1.0 + scale)[None] + shift[None]).reshape(rows, d)


@functools.lru_cache(maxsize=None)
def _dft_consts():
    n = FFT_N1 * FFT_N2
    half1 = FFT_N1 // 2
    k1 = np.arange(FFT_N1)[:, None]
    n1 = np.arange(half1)[None, :]
    ang = 2.0 * np.pi * k1 * n1 / FFT_N1
    c, s = np.cos(ang), np.sin(ang)
    m1 = np.block([[c, s], [-s, c]])
    k2 = np.arange(FFT_N2)[None, :, None]
    n2 = np.arange(FFT_N2)[None, None, :]
    kk1 = np.arange(FFT_N1)[:, None, None]
    th = 2.0 * np.pi * (n2 * k2 / FFT_N2 + n2 * kk1 / n)
    fr, fi = np.cos(th), -np.sin(th)
    g2 = np.concatenate([np.concatenate([fr, -fi], axis=2), np.concatenate([fi, fr], axis=2)], axis=1)
    g2t = np.transpose(g2, (0, 2, 1))
    cm, sm = c.T / n, s.T / n
    m3 = np.block([[cm, -sm], [sm, cm]])
    f32 = lambda a: np.asarray(a, np.float32)
    return f32(m1), f32(g2), f32(g2t), f32(m3)


@functools.lru_cache(maxsize=None)
def _dft_consts_direct(length):
    n = 2 * length
    f = np.arange(n)[:, None]
    t = np.arange(length)[None, :]
    ang = 2.0 * np.pi * f * t / n
    c, s = np.cos(ang), np.sin(ang)
    fwd = np.block([[c, s], [-s, c]])
    ct, st = c.T / n, s.T / n
    inv = np.block([[ct, -st], [st, ct]])
    f32 = lambda a: np.asarray(a, np.float32)
    return f32(fwd), f32(inv), f32(c), f32(-s)


def _hyena_deltas():
    min_decay = math.log(HYENA_DECAY_TARGET) / HYENA_DECAY_SLOW
    max_decay = math.log(HYENA_DECAY_TARGET) / HYENA_DECAY_FAST
    return jnp.abs(jnp.linspace(min_decay, max_decay, D_HYENA, dtype=F32))


def _pos_feats(length, grid):
    d = jnp.arange(length, dtype=jnp.int32)
    t = d.astype(F32) / length
    bands = jnp.arange(1, N_BANDS + 1, dtype=F32)
    ang = 2.0 * math.pi * t[:, None] * bands[None, :]
    feats = [t[:, None], jnp.sin(ang), jnp.cos(ang)]
    if grid:
        rows = max(length // GRID_W, 1)
        col_ang = 2.0 * math.pi * (d % GRID_W).astype(F32) / GRID_W
        row = (d // GRID_W).astype(F32) / rows
        feats += [jnp.sin(col_ang)[:, None], jnp.cos(col_ang)[:, None], row[:, None]]
    else:
        feats.append(jnp.zeros((length, 3), F32))
    z = jnp.concatenate(feats, axis=-1)
    return jnp.pad(z, ((0, 0), (0, LANES - POS_DIM)))


def _ada_kernel(cv_ref, w_ref, b_ref, o_ref):
    s = _silu(cv_ref[...])
    o_ref[...] = _dot(s, w_ref[...]) + b_ref[...]


def _ada_mods(cv, ada_w, ada_b):
    tn = D_MODEL
    return pl.pallas_call(
        _ada_kernel,
        out_shape=jax.ShapeDtypeStruct((DEPTH, 16, 6 * D_MODEL), F32),
        grid=(DEPTH, 6 * D_MODEL // tn),
        in_specs=[
            pl.BlockSpec((16, D_MODEL), lambda l, j: (0, 0)),
            pl.BlockSpec((None, D_MODEL, tn), lambda l, j: (l, 0, j)),
            pl.BlockSpec((None, 1, tn), lambda l, j: (l, 0, j)),
        ],
        out_specs=pl.BlockSpec((None, 16, tn), lambda l, j: (l, 0, j)),
        compiler_params=_cparams(("arbitrary", "arbitrary")),
        name="ada_mods",
    )(cv, ada_w, ada_b.reshape(DEPTH, 1, 6 * D_MODEL))


def _in_kernel(x_ref, g_ref, mod_ref, w_ref, o_ref):
    y = _rms(x_ref[...], g_ref[...])
    h = _modulate(y, mod_ref[:, 0:D_MODEL], mod_ref[:, D_MODEL:2 * D_MODEL])
    o_ref[...] = jnp.dot(h.astype(BF16), w_ref[...], preferred_element_type=F32)


def _in_proj(x, norm_mix, mods, w_in_bf, l, seg, tm=512):
    rows = x.shape[0]
    return pl.pallas_call(
        _in_kernel,
        out_shape=jax.ShapeDtypeStruct((rows, N_IN), F32),
        grid=(rows // tm,),
        in_specs=[
            pl.BlockSpec((tm, D_MODEL), lambda i: (i, 0)),
            pl.BlockSpec((None, 1, D_MODEL), lambda i: (l, 0, 0)),
            pl.BlockSpec((None, SUBLANES, 2 * D_MODEL), lambda i: (l, seg, 0)),
            pl.BlockSpec((None, D_MODEL, N_IN), lambda i: (l, 0, 0)),
        ],
        out_specs=pl.BlockSpec((tm, N_IN), lambda i: (i, 0)),
        compiler_params=_cparams(("arbitrary",)),
        name="in_proj",
    )(x, norm_mix.reshape(DEPTH, 1, D_MODEL), mods, w_in_bf)


def _conv_kernel(main_ref, prev_ref, next_ref, sw_ref, sb_ref, cw_ref, cb_ref,
                 u_ref, x2_ref, xr_ref, *, batch, tm):
    i = pl.program_id(0)
    last = pl.num_programs(0) - 1
    keep_prev = (i > 0).astype(F32)
    keep_next = (i < last).astype(F32)

    def tap(d, c0, c1):
        if d == 0:
            return main_ref[:, c0:c1]
        if d < 0:
            k = -d * batch
            head = prev_ref[2 * batch - k:2 * batch, c0:c1] * keep_prev
            return jnp.concatenate([head, main_ref[0:tm - k, c0:c1]], axis=0)
        k = d * batch
        tail = next_ref[0:k, c0:c1] * keep_next
        return jnp.concatenate([main_ref[k:tm, c0:c1], tail], axis=0)

    def hyena_branch(j):
        c0, c1 = j * D_HYENA, (j + 1) * D_HYENA
        acc = sb_ref[:, c0:c1] + tap(-1, c0, c1) * sw_ref[0:1, c0:c1]
        acc = acc + tap(0, c0, c1) * sw_ref[1:2, c0:c1]
        return acc + tap(1, c0, c1) * sw_ref[2:3, c0:c1]

    u_ref[...] = hyena_branch(0) * hyena_branch(1)
    x2_ref[...] = hyena_branch(2)
    c0, c1 = 3 * D_HYENA, 3 * D_HYENA + D_LRU
    acc = cb_ref[...] + tap(-2, c0, c1) * cw_ref[0:1, :]
    acc = acc + tap(-1, c0, c1) * cw_ref[1:2, :]
    acc = acc + tap(0, c0, c1) * cw_ref[2:3, :]
    xr_ref[...] = acc + tap(1, c0, c1) * cw_ref[3:4, :]


def _short_convs(proj, short_w, short_b, conv_w, conv_b, l, batch, tm=512):
    rows = proj.shape[0]
    width = 3 * D_HYENA + D_LRU
    per_prev = tm // (2 * batch)
    per_next = tm // batch
    n_next = rows // batch
    out = jax.ShapeDtypeStruct((rows, D_HYENA), F32)
    return pl.pallas_call(
        functools.partial(_conv_kernel, batch=batch, tm=tm),
        out_shape=(out, out, out),
        grid=(rows // tm,),
        in_specs=[
            pl.BlockSpec((tm, width), lambda i: (i, 0)),
            pl.BlockSpec((2 * batch, width), lambda i: (jnp.maximum(i * per_prev - 1, 0), 0)),
            pl.BlockSpec((batch, width), lambda i: (jnp.minimum((i + 1) * per_next, n_next - 1), 0)),
            pl.BlockSpec((None, 3, 3 * D_HYENA), lambda i: (l, 0, 0)),
            pl.BlockSpec((None, 1, 3 * D_HYENA), lambda i: (l, 0, 0)),
            pl.BlockSpec((None, 4, D_LRU), lambda i: (l, 0, 0)),
            pl.BlockSpec((None, 1, D_LRU), lambda i: (l, 0, 0)),
        ],
        out_specs=(pl.BlockSpec((tm, D_HYENA), lambda i: (i, 0)),) * 3,
        compiler_params=_cparams(("arbitrary",)),
        name="short_convs",
    )(proj, proj, proj, short_w, short_b.reshape(DEPTH, 1, 3 * D_HYENA),
      conv_w, conv_b.reshape(DEPTH, 1, D_LRU))


def _lru_gates(xr, w_ref, ba, bx, sp):
    xb = xr.astype(BF16)
    half = D_LRU // 2
    ra, rx = [], []
    for hf in range(2):
        xs = xb[:, hf * half:(hf + 1) * half]
        ra.append(jnp.dot(xs, w_ref[0, hf], preferred_element_type=F32))
        rx.append(jnp.dot(xs, w_ref[1, hf], preferred_element_type=F32))
    gate_a = jax.nn.sigmoid(jnp.concatenate(ra, axis=1) + ba)
    gate_x = jax.nn.sigmoid(jnp.concatenate(rx, axis=1) + bx)
    log_a = (-LRU_C) * gate_a * sp
    a = jnp.exp(log_a)
    mult = jnp.sqrt(jnp.maximum(jnp.tanh(-log_a) * (1.0 + a * a), 0.0))
    return a, xr * gate_x * mult


def _softplus(x):
    return jnp.maximum(x, 0.0) + jnp.log1p(jnp.exp(-jnp.abs(x)))


def _lru_kernel(*refs, batch, tm, chunk, reverse):
    if reverse:
        (xr_ref, w_ref, ba_ref, bx_ref, lam_ref, h0_ref, hf_ref, pg_ref,
         y_ref, hl_ref, a_scr, b_scr, h_scr) = refs
    else:
        (xr_ref, w_ref, ba_ref, bx_ref, lam_ref, h0_ref,
         y_ref, hl_ref, a_scr, b_scr, h_scr) = refs
    i = pl.program_id(0)

    @pl.when(i == 0)
    def _():
        h_scr[...] = h0_ref[...].astype(F32)

    sp = _softplus(-lam_ref[...])
    ba = ba_ref[...]
    bx = bx_ref[...]

    def fill(c, carry):
        r = pl.multiple_of(c * chunk, chunk)
        a, b = _lru_gates(xr_ref[pl.ds(r, chunk), :], w_ref, ba, bx, sp)
        a_scr[pl.ds(r, chunk), :] = a
        b_scr[pl.ds(r, chunk), :] = b
        return carry

    lax.fori_loop(0, tm // chunk, fill, 0)

    steps = tm // batch

    def step(s, h):
        t = (steps - 1 - s) if reverse else s
        r = pl.multiple_of(t * batch, batch)
        h = a_scr[pl.ds(r, batch), :] * h + b_scr[pl.ds(r, batch), :]
        if reverse:
            y_ref[pl.ds(r, batch), :] = ((hf_ref[pl.ds(r, batch), :] + h)
                                         * _gelu_tanh(pg_ref[pl.ds(r, batch), :]))
        else:
            y_ref[pl.ds(r, batch), :] = h
        return h

    h = lax.fori_loop(0, steps, step, h_scr[...], unroll=8)
    h_scr[...] = h

    @pl.when(i == pl.num_programs(0) - 1)
    def _():
        hl_ref[...] = h


def _lru_scan(xr, wbd, lru_ba, lru_bx, lru_lambda, h0, h0_block, l, direction, batch,
              hf=None, proj=None, tm=2048, chunk=256):
    rows = xr.shape[0]
    nt = rows // tm
    reverse = direction == 1
    row_map = (lambda i: (nt - 1 - i, 0)) if reverse else (lambda i: (i, 0))
    vec = lambda: pl.BlockSpec((None, None, 1, D_LRU), lambda i: (l, direction, 0, 0))
    in_specs = [
        pl.BlockSpec((tm, D_LRU), row_map),
        pl.BlockSpec((None, None, 2, 2, D_LRU // 2, D_LRU // 2), lambda i: (l, direction, 0, 0, 0, 0)),
        vec(), vec(), vec(),
        pl.BlockSpec((batch, D_LRU), lambda i: h0_block),
    ]
    args = [xr, wbd, lru_ba.reshape(DEPTH, 2, 1, D_LRU), lru_bx.reshape(DEPTH, 2, 1, D_LRU),
            lru_lambda.reshape(DEPTH, 2, 1, D_LRU), h0]
    if reverse:
        in_specs += [pl.BlockSpec((tm, D_LRU), row_map),
                     pl.BlockSpec((tm, D_LRU), lambda i: (nt - 1 - i, (3 * D_HYENA + D_LRU) // D_LRU))]
        args += [hf, proj]
    return pl.pallas_call(
        functools.partial(_lru_kernel, batch=batch, tm=tm, chunk=chunk, reverse=reverse),
        out_shape=(jax.ShapeDtypeStruct((rows, D_LRU), F32), jax.ShapeDtypeStruct((batch, D_LRU), F32)),
        grid=(nt,),
        in_specs=in_specs,
        out_specs=(pl.BlockSpec((tm, D_LRU), row_map), pl.BlockSpec((batch, D_LRU), lambda i: (0, 0))),
        scratch_shapes=[pltpu.VMEM((tm, D_LRU), F32), pltpu.VMEM((tm, D_LRU), F32),
                        pltpu.VMEM((batch, D_LRU), F32)],
        compiler_params=_cparams(("arbitrary",)),
        name="lru_rev" if reverse else "lru_fwd",
    )(*args)


def _filter_kernel(z_ref, w1_ref, b1_ref, w2_ref, b2_ref, w3_ref, b3_ref, fr_ref, dl_ref,
                   ks_ref, kd_ref, sum_ref):
    i = pl.program_id(0)
    hi = lax.Precision.HIGHEST
    z = z_ref[...]
    h1 = jnp.sin(fr_ref[0:1, :] * (jnp.dot(z, w1_ref[...], precision=hi, preferred_element_type=F32)
                                   + b1_ref[...]))
    h2 = jnp.sin(fr_ref[1:2, :] * (jnp.dot(h1, w2_ref[...], precision=hi, preferred_element_type=F32)
                                   + b2_ref[...]))
    k = jnp.dot(h2, w3_ref[...], precision=hi, preferred_element_type=F32) + b3_ref[...]
    window = jnp.exp(-z[:, 0:1] * dl_ref[...]) + HYENA_SHIFT
    kf = k[:, :D_HYENA] * window
    kb = k[:, D_HYENA:] * window
    row = lax.broadcasted_iota(jnp.int32, kb.shape, 0) + i * kb.shape[0]
    kb = jnp.where(row == 0, 0.0, kb)
    ks_ref[...] = kf + kb
    kd_ref[...] = kf - kb
    part = jnp.sum(jnp.abs(kf) + jnp.abs(kb), axis=0, keepdims=True)

    @pl.when(i == 0)
    def _():
        sum_ref[...] = part

    @pl.when(i > 0)
    def _():
        sum_ref[...] += part


def _hyena_filter_taps(length, grid, p, l, tl=256):
    z = _pos_feats(length, grid)
    w1 = jnp.pad(p['filt_w1'], ((0, 0), (0, LANES - POS_DIM), (0, 0)))
    tap = jax.ShapeDtypeStruct((length, D_HYENA), F32)
    h = FILTER_HIDDEN
    return pl.pallas_call(
        _filter_kernel,
        out_shape=(tap, tap, jax.ShapeDtypeStruct((1, D_HYENA), F32)),
        grid=(length // tl,),
        in_specs=[
            pl.BlockSpec((tl, LANES), lambda i: (i, 0)),
            pl.BlockSpec((None, LANES, h), lambda i: (l, 0, 0)),
            pl.BlockSpec((None, 1, h), lambda i: (l, 0, 0)),
            pl.BlockSpec((None, h, h), lambda i: (l, 0, 0)),
            pl.BlockSpec((None, 1, h), lambda i: (l, 0, 0)),
            pl.BlockSpec((None, h, 2 * D_HYENA), lambda i: (l, 0, 0)),
            pl.BlockSpec((None, 1, 2 * D_HYENA), lambda i: (l, 0, 0)),
            pl.BlockSpec((None, 2, h), lambda i: (l, 0, 0)),
            pl.BlockSpec((1, D_HYENA), lambda i: (0, 0)),
        ],
        out_specs=(pl.BlockSpec((tl, D_HYENA), lambda i: (i, 0)),
                   pl.BlockSpec((tl, D_HYENA), lambda i: (i, 0)),
                   pl.BlockSpec((1, D_HYENA), lambda i: (0, 0))),
        compiler_params=_cparams(("arbitrary",)),
        name="hyena_filter",
    )(z, w1, p['filt_b1'].reshape(DEPTH, 1, h), p['filt_w2'], p['filt_b2'].reshape(DEPTH, 1, h),
      p['filt_w3'], p['filt_b3'].reshape(DEPTH, 1, 2 * D_HYENA), p['filt_freq'],
      _hyena_deltas().reshape(1, D_HYENA))


def _spec_direct_kernel(c_ref, s_ref, ks_ref, kd_ref, sum_ref, o_ref):
    scale = 1.0 / (sum_ref[...] + EPS)
    o_ref[0] = _dot(c_ref[...], ks_ref[...], x3=True) * scale
    o_ref[1] = _dot(s_ref[...], kd_ref[...], x3=True) * scale


def _filter_spectrum_direct(ks, kd, ksum):
    length = ks.shape[0]
    n = 2 * length
    _, _, c, ms = _dft_consts_direct(length)
    return pl.pallas_call(
        _spec_direct_kernel,
        out_shape=jax.ShapeDtypeStruct((2, n, D_HYENA), F32),
        compiler_params=_cparams(()),
        name="filter_spectrum_direct",
    )(jnp.asarray(c), jnp.asarray(ms), ks, kd, ksum)


def _conv_direct_kernel(u_ref, x2_ref, kh_ref, fwd_ref, inv_ref, bias_ref, o_ref, *, length, batch):
    n = 2 * length
    kr, ki = kh_ref[0], kh_ref[1]
    for q in range(batch // 2):
        z = jnp.concatenate([u_ref[:, 2 * q, :], u_ref[:, 2 * q + 1, :]], axis=0)
        x = _dot(fwd_ref[...], z, x3=True)
        xr, xi = x[:n], x[n:]
        y = jnp.concatenate([xr * kr - xi * ki, xr * ki + xi * kr], axis=0)
        conv = _dot(inv_ref[...], y, x3=True)
        o_ref[:, 2 * q, :] = conv[:length]
        o_ref[:, 2 * q + 1, :] = conv[length:]
    o_ref[...] = x2_ref[...] * (o_ref[...] + u_ref[...] * bias_ref[...])


def _hyena_conv_direct(u, x2, khat, hyena_bias, l, batch, cw=256):
    rows = u.shape[0]
    length = rows // batch
    fwd, inv, _, _ = _dft_consts_direct(length)
    blk = pl.BlockSpec((length, batch, cw), lambda j: (0, 0, j))
    out = pl.pallas_call(
        functools.partial(_conv_direct_kernel, length=length, batch=batch),
        out_shape=jax.ShapeDtypeStruct((length, batch, D_HYENA), F32),
        grid=(D_HYENA // cw,),
        in_specs=[blk, blk,
                  pl.BlockSpec((2, 2 * length, cw), lambda j: (0, 0, j)),
                  pl.BlockSpec(fwd.shape, lambda j: (0, 0)),
                  pl.BlockSpec(inv.shape, lambda j: (0, 0)),
                  pl.BlockSpec((None, 1, cw), lambda j: (l, 0, j))],
        out_specs=blk,
        compiler_params=_cparams(("arbitrary",)),
        name="hyena_conv_direct",
    )(u.reshape(length, batch, D_HYENA), x2.reshape(length, batch, D_HYENA), khat,
      jnp.asarray(fwd), jnp.asarray(inv), hyena_bias.reshape(DEPTH, 1, D_HYENA))
    return out.reshape(rows, D_HYENA)


def _stage1_kernel(x_ref, m_ref, o_ref, *, batch):
    m = m_ref[...].astype(BF16)
    for j in range(N2_BLOCK):
        for q in range(batch // 2):
            z = jnp.concatenate([x_ref[:, j, 2 * q, :], x_ref[:, j, 2 * q + 1, :]], axis=0)
            r = jnp.dot(m, z.astype(BF16), preferred_element_type=F32)
            o_ref[0, :, q, j, :] = r[:FFT_N1]
            o_ref[1, :, q, j, :] = r[FFT_N1:]


def _fft_stage1(u, batch, cw=256):
    c = u.shape[1]
    half1 = FFT_N1 // 2
    m1, _, _, _ = _dft_consts()
    return pl.pallas_call(
        functools.partial(_stage1_kernel, batch=batch),
        out_shape=jax.ShapeDtypeStruct((2, FFT_N1, batch // 2, FFT_N2, c), F32),
        grid=(FFT_N2 // N2_BLOCK, c // cw),
        in_specs=[pl.BlockSpec((half1, N2_BLOCK, batch, cw), lambda j, k: (0, j, 0, k)),
                  pl.BlockSpec(m1.shape, lambda j, k: (0, 0))],
        out_specs=pl.BlockSpec((2, FFT_N1, batch // 2, N2_BLOCK, cw), lambda j, k: (0, 0, 0, j, k)),
        compiler_params=_cparams(("arbitrary", "arbitrary")),
        name="fft_stage1",
    )(u.reshape(half1, FFT_N2, batch, c), jnp.asarray(m1))


def _stage1_filter_kernel(ks_ref, kd_ref, m_ref, o_ref):
    m = m_ref[:, :FFT_N1 // 2]
    for j in range(N2_BLOCK):
        for g, ref in enumerate((ks_ref, kd_ref)):
            r = _dot(m, ref[:, j, :], x3=True)
            o_ref[0, :, g, j, :] = r[:FFT_N1]
            o_ref[1, :, g, j, :] = r[FFT_N1:]


def _stage2_filter_kernel(a_ref, g_ref, sum_ref, o_ref):
    scale = 1.0 / (sum_ref[...] + EPS)
    for j in range(K1_BLOCK):
        bs = jnp.concatenate([a_ref[0, j, 0], a_ref[1, j, 0]], axis=0)
        bd = jnp.concatenate([a_ref[0, j, 1], a_ref[1, j, 1]], axis=0)
        o_ref[0, j] = _dot(g_ref[j], bs, x3=True)[:FFT_N2] * scale
        o_ref[1, j] = _dot(g_ref[j], bd, x3=True)[FFT_N2:] * scale


def _filter_spectrum_2stage(ks, kd, ksum):
    c = D_HYENA
    half1 = FFT_N1 // 2
    m1, g2, _, _ = _dft_consts()
    tap = pl.BlockSpec((half1, N2_BLOCK, c), lambda j: (0, j, 0))
    a = pl.pallas_call(
        _stage1_filter_kernel,
        out_shape=jax.ShapeDtypeStruct((2, FFT_N1, 2, FFT_N2, c), F32),
        grid=(FFT_N2 // N2_BLOCK,),
        in_specs=[tap, tap, pl.BlockSpec(m1.shape, lambda j: (0, 0))],
        out_specs=pl.BlockSpec((2, FFT_N1, 2, N2_BLOCK, c), lambda j: (0, 0, 0, j, 0)),
        compiler_params=_cparams(("arbitrary",)),
        name="filter_spectrum_stage1",
    )(ks.reshape(half1, FFT_N2, c), kd.reshape(half1, FFT_N2, c), jnp.asarray(m1))
    return pl.pallas_call(
        _stage2_filter_kernel,
        out_shape=jax.ShapeDtypeStruct((2, FFT_N1, FFT_N2, c), F32),
        grid=(FFT_N1 // K1_BLOCK,),
        in_specs=[pl.BlockSpec((2, K1_BLOCK, 2, FFT_N2, c), lambda i: (0, i, 0, 0, 0)),
                  pl.BlockSpec((K1_BLOCK, 2 * FFT_N2, 2 * FFT_N2), lambda i: (i, 0, 0)),
                  pl.BlockSpec((1, c), lambda i: (0, 0))],
        out_specs=pl.BlockSpec((2, K1_BLOCK, FFT_N2, c), lambda i: (0, i, 0, 0)),
        compiler_params=_cparams(("arbitrary",)),
        name="filter_spectrum_stage2",
    )(a, jnp.asarray(g2), ksum)


def _stage2_kernel(a_ref, kh_ref, g_ref, gt_ref, o_ref):
    for j in range(K1_BLOCK):
        b = jnp.concatenate([a_ref[0, j], a_ref[1, j]], axis=0)
        x = _dot(g_ref[j], b)
        xr, xi = x[:FFT_N2], x[FFT_N2:]
        kr, ki = kh_ref[0, j], kh_ref[1, j]
        y = jnp.concatenate([xr * kr - xi * ki, xr * ki + xi * kr], axis=0)
        r = _dot(gt_ref[j], y)
        o_ref[0, j] = r[:FFT_N2]
        o_ref[1, j] = r[FFT_N2:]


def _fft_stage2(a, khat):
    _, g2, g2t, _ = _dft_consts()
    c = D_HYENA
    pairs = a.shape[2]
    ablk = pl.BlockSpec((2, K1_BLOCK, None, FFT_N2, c), lambda i, q: (0, i, q, 0, 0))
    gblk = pl.BlockSpec((K1_BLOCK, 2 * FFT_N2, 2 * FFT_N2), lambda i, q: (i, 0, 0))
    return pl.pallas_call(
        _stage2_kernel,
        out_shape=jax.ShapeDtypeStruct(a.shape, F32),
        grid=(FFT_N1 // K1_BLOCK, pairs),
        in_specs=[ablk, pl.BlockSpec((2, K1_BLOCK, FFT_N2, c), lambda i, q: (0, i, 0, 0)), gblk, gblk],
        out_specs=ablk,
        compiler_params=_cparams(("arbitrary", "arbitrary")),
        name="fft_stage2",
    )(a, khat, jnp.asarray(g2), jnp.asarray(g2t))


def _stage3_kernel(a_ref, m_ref, u_ref, x2_ref, bias_ref, o_ref, *, batch):
    half1 = FFT_N1 // 2
    m = m_ref[...].astype(BF16)
    for j in range(N2_BLOCK):
        for q in range(batch // 2):
            b = jnp.concatenate([a_ref[0, :, q, j, :], a_ref[1, :, q, j, :]], axis=0)
            y = jnp.dot(m, b.astype(BF16), preferred_element_type=F32)
            o_ref[:, j, 2 * q, :] = y[:half1]
            o_ref[:, j, 2 * q + 1, :] = y[half1:]
    o_ref[...] = x2_ref[...] * (o_ref[...] + u_ref[...] * bias_ref[...])


def _fft_stage3(a, u, x2, hyena_bias, l, batch, cw=256):
    rows, c = u.shape
    half1 = FFT_N1 // 2
    _, _, _, m3 = _dft_consts()
    blk = pl.BlockSpec((half1, N2_BLOCK, batch, cw), lambda j, k: (0, j, 0, k))
    out = pl.pallas_call(
        functools.partial(_stage3_kernel, batch=batch),
        out_shape=jax.ShapeDtypeStruct((half1, FFT_N2, batch, c), F32),
        grid=(FFT_N2 // N2_BLOCK, c // cw),
        in_specs=[pl.BlockSpec((2, FFT_N1, batch // 2, N2_BLOCK, cw), lambda j, k: (0, 0, 0, j, k)),
                  pl.BlockSpec(m3.shape, lambda j, k: (0, 0)),
                  blk, blk,
                  pl.BlockSpec((None, 1, cw), lambda j, k: (l, 0, k))],
        out_specs=blk,
        compiler_params=_cparams(("arbitrary", "arbitrary")),
        name="fft_stage3",
    )(a, jnp.asarray(m3), u.reshape(half1, FFT_N2, batch, c), x2.reshape(half1, FFT_N2, batch, c),
      hyena_bias.reshape(DEPTH, 1, c))
    return out.reshape(rows, c)


def _hyena_conv_2stage(u, x2, khat, hyena_bias, l, batch):
    a = _fft_stage1(u, batch)
    a = _fft_stage2(a, khat)
    return _fft_stage3(a, u, x2, hyena_bias, l, batch)


def _out_kernel(hy_ref, lr_ref, x_ref, w_ref, g_ref, mod_ref, xo_ref, h_ref):
    d = D_MODEL
    mixed = (jnp.dot(hy_ref[...].astype(BF16), w_ref[0:D_HYENA, :], preferred_element_type=F32)
             + jnp.dot(lr_ref[...].astype(BF16), w_ref[D_HYENA:, :], preferred_element_type=F32))
    rows = mixed.shape[0]
    g1 = mod_ref[:, 2 * d:3 * d]
    x = x_ref[...] + (mixed.reshape(rows // SUBLANES, SUBLANES, d) * g1[None]).reshape(rows, d)
    xo_ref[...] = x
    h = _modulate(_rms(x, g_ref[...]), mod_ref[:, 3 * d:4 * d], mod_ref[:, 4 * d:5 * d])
    h_ref[...] = h.astype(BF16)


def _out_proj(hy, lr, x, w_out_bf, norm_ffn, mods, l, seg, tm=512):
    rows = x.shape[0]
    d = D_MODEL
    return pl.pallas_call(
        _out_kernel,
        out_shape=(jax.ShapeDtypeStruct((rows, d), F32), jax.ShapeDtypeStruct((rows, d), BF16)),
        grid=(rows // tm,),
        in_specs=[
            pl.BlockSpec((tm, D_HYENA), lambda i: (i, 0)),
            pl.BlockSpec((tm, D_LRU), lambda i: (i, 0)),
            pl.BlockSpec((tm, d), lambda i: (i, 0)),
            pl.BlockSpec((None, D_HYENA + D_LRU, d), lambda i: (l, 0, 0)),
            pl.BlockSpec((None, 1, d), lambda i: (l, 0, 0)),
            pl.BlockSpec((None, SUBLANES, 6 * d), lambda i: (l, seg, 0)),
        ],
        out_specs=(pl.BlockSpec((tm, d), lambda i: (i, 0)), pl.BlockSpec((tm, d), lambda i: (i, 0))),
        compiler_params=_cparams(("arbitrary",)),
        name="out_proj",
    )(hy, lr, x, w_out_bf, norm_ffn.reshape(DEPTH, 1, d), mods)


def _gate_residual(x, acc, g2):
    rows, d = acc.shape
    return x + (acc.reshape(rows // SUBLANES, SUBLANES, d) * g2[None]).reshape(rows, d)


def _ffn_kernel(h_ref, x_ref, wg_ref, wu_ref, wd_ref, mod_ref, o_ref, acc_ref):
    f = pl.program_id(1)
    h = h_ref[...]
    g = jnp.dot(h, wg_ref[...], preferred_element_type=F32)
    u = jnp.dot(h, wu_ref[...], preferred_element_type=F32)
    part = jnp.dot((_silu(g) * u).astype(BF16), wd_ref[...], preferred_element_type=F32)

    @pl.when(f == 0)
    def _():
        acc_ref[...] = part

    @pl.when(f > 0)
    def _():
        acc_ref[...] += part

    @pl.when(f == pl.num_programs(1) - 1)
    def _():
        o_ref[...] = _gate_residual(x_ref[...], acc_ref[...], mod_ref[...])


def _ffn_dense(h, x, w_gu_bf, w_down_bf, mods, m, l, seg, tm=512, fc=1408):
    rows = x.shape[0]
    d = D_MODEL
    nf = D_FF // fc
    return pl.pallas_call(
        _ffn_kernel,
        out_shape=jax.ShapeDtypeStruct((rows, d), F32),
        grid=(rows // tm, nf),
        in_specs=[
            pl.BlockSpec((tm, d), lambda i, f: (i, 0)),
            pl.BlockSpec((tm, d), lambda i, f: (i, 0)),
            pl.BlockSpec((None, d, fc), lambda i, f: (m, 0, f)),
            pl.BlockSpec((None, d, fc), lambda i, f: (m, 0, nf + f)),
            pl.BlockSpec((None, fc, d), lambda i, f: (m, f, 0)),
            pl.BlockSpec((None, SUBLANES, d), lambda i, f: (l, seg, 5)),
        ],
        out_specs=pl.BlockSpec((tm, d), lambda i, f: (i, 0)),
        scratch_shapes=[pltpu.VMEM((tm, d), F32)],
        compiler_params=_cparams(("arbitrary", "arbitrary")),
        name="ffn_dense",
    )(h, x, w_gu_bf, w_gu_bf, w_down_bf, mods)


def _route_top2(logits):
    lane = lax.broadcasted_iota(jnp.int32, logits.shape, 1)
    neg = jnp.float32(-jnp.inf)
    v = jnp.where(lane < N_EXPERTS, logits, neg)
    m1 = jnp.max(v, axis=-1, keepdims=True)
    i1 = jnp.min(jnp.where(v == m1, lane, LANES), axis=-1, keepdims=True)
    v2 = jnp.where(lane == i1, neg, v)
    m2 = jnp.max(v2, axis=-1, keepdims=True)
    i2 = jnp.min(jnp.where(v2 == m2, lane, LANES), axis=-1, keepdims=True)
    e2 = jnp.exp(m2 - m1)
    w1 = 1.0 / (1.0 + e2)
    w2 = e2 / (1.0 + e2)
    first = lane == i1
    second = lane == i2
    comb = jnp.where(first, w1, 0.0) + jnp.where(second, w2, 0.0)
    return comb, jnp.where(first | second, 1.0, 0.0)


MOE_TILE = 2048
MOE_CH = 256
MOE_SB = 128
MOE_CS_ROWS = 16


def _router_kernel(h_ref, r_ref, comb_ref, key_ref, keyt_ref, cs_ref, *, tt):
    nch = tt // MOE_CH
    logits = _dot(h_ref[...].astype(F32), r_ref[...], x3=True)
    comb, sel = _route_top2(logits)
    comb_ref[...] = comb
    rr = lax.broadcasted_iota(jnp.int32, (MOE_CH, MOE_CH), 0)
    cc = lax.broadcasted_iota(jnp.int32, (MOE_CH, MOE_CH), 1)
    strict_lower = jnp.where(cc < rr, 1.0, 0.0).astype(BF16)
    cs_ref[...] = jnp.zeros(cs_ref.shape, jnp.int32)
    carry = jnp.zeros((1, LANES), F32)
    for c in range(nch):
        selc = sel[c * MOE_CH:(c + 1) * MOE_CH]
        cs_ref[c:c + 1, :] = carry.astype(jnp.int32)
        rank = jnp.dot(strict_lower, selc.astype(BF16), preferred_element_type=F32) + carry
        key_ref[c * MOE_CH:(c + 1) * MOE_CH, :] = jnp.where(selc > 0.0, rank, -1.0)
        carry = carry + jnp.sum(selc, axis=0, keepdims=True)
    cs_ref[nch:nch + 1, :] = carry.astype(jnp.int32)
    keyt = jnp.transpose(key_ref[...])
    for e in range(N_EXPERTS):
        for c in range(nch):
            keyt_ref[e, c] = keyt[e:e + 1, c * MOE_CH:(c + 1) * MOE_CH]


def _moe_route(h, router_pad, m, tt):
    rows, d = h.shape
    nt = rows // tt
    nch = tt // MOE_CH
    tok = jax.ShapeDtypeStruct((rows, LANES), F32)
    return pl.pallas_call(
        functools.partial(_router_kernel, tt=tt),
        out_shape=(tok, tok,
                   jax.ShapeDtypeStruct((N_EXPERTS, rows // MOE_CH, 1, MOE_CH), F32),
                   jax.ShapeDtypeStruct((nt * MOE_CS_ROWS, LANES), jnp.int32)),
        grid=(nt,),
        in_specs=[pl.BlockSpec((tt, d), lambda i: (i, 0)),
                  pl.BlockSpec((None, d, LANES), lambda i: (m, 0, 0))],
        out_specs=(pl.BlockSpec((tt, LANES), lambda i: (i, 0)),
                   pl.BlockSpec((tt, LANES), lambda i: (i, 0)),
                   pl.BlockSpec((N_EXPERTS, nch, 1, MOE_CH), lambda i: (0, i, 0, 0)),
                   pl.BlockSpec((MOE_CS_ROWS, LANES), lambda i: (i, 0))),
        compiler_params=_cparams(("arbitrary",)),
        name="moe_route",
    )(h, router_pad)


def _moe_kernel(cs_ref, h_ref, x_ref, comb_ref, key_ref, keyt_ref, wg_ref, wu_ref, wd_ref, mod_ref,
                o_ref, xg_ref, gacc_ref, *, tt):
    i = pl.program_id(0)
    e = pl.program_id(1)
    f = pl.program_id(2)
    nch = tt // MOE_CH
    base = i * ((nch + 1) * N_EXPERTS) + e

    def count_before(c):
        return cs_ref[base + c * N_EXPERTS]

    n_sub = (count_before(nch) + (MOE_SB - 1)) // MOE_SB

    def chunk_range(j):
        lo = j * MOE_SB
        hi = lo + MOE_SB
        c_lo = jnp.int32(0)
        c_hi = jnp.int32(0)
        for c in range(nch):
            c_lo = c_lo + (count_before(c + 1) <= lo).astype(jnp.int32)
            c_hi = c_hi + (count_before(c) < hi).astype(jnp.int32)
        return c_lo, c_hi

    @pl.when((e == 0) & (f == 0))
    def _():
        o_ref[...] = jnp.zeros_like(o_ref)

    @pl.when(f == 0)
    def _():
        def gather_sub(j, carry):
            c_lo, c_hi = chunk_range(j)
            gacc_ref[...] = jnp.zeros_like(gacc_ref)
            row = (lax.broadcasted_iota(jnp.int32, (MOE_SB, MOE_CH), 0) + j * MOE_SB).astype(F32)

            def gather_chunk(c, carry2):
                t0 = pl.multiple_of(c * MOE_CH, MOE_CH)
                sel = jnp.where(keyt_ref[e, c] == row, 1.0, 0.0).astype(BF16)
                gacc_ref[...] += jnp.dot(sel, h_ref[pl.ds(t0, MOE_CH), :], preferred_element_type=F32)
                return carry2

            lax.fori_loop(c_lo, c_hi, gather_chunk, 0)
            xg_ref[pl.ds(pl.multiple_of(j * MOE_SB, MOE_SB), MOE_SB), :] = gacc_ref[...].astype(BF16)
            return carry

        lax.fori_loop(0, n_sub, gather_sub, 0)

    lane_e = lax.broadcasted_iota(jnp.int32, (MOE_CH, LANES), 1) == e

    def expert_sub(j, carry):
        xg = xg_ref[pl.ds(pl.multiple_of(j * MOE_SB, MOE_SB), MOE_SB), :]
        g = jnp.dot(xg, wg_ref[...], preferred_element_type=F32)
        u = jnp.dot(xg, wu_ref[...], preferred_element_type=F32)
        y = jnp.dot((_silu(g) * u).astype(BF16), wd_ref[...], preferred_element_type=F32).astype(BF16)
        c_lo, c_hi = chunk_range(j)
        pos = (lax.broadcasted_iota(jnp.int32, (MOE_CH, MOE_SB), 1) + j * MOE_SB).astype(F32)

        def scatter_chunk(c, carry2):
            t0 = pl.multiple_of(c * MOE_CH, MOE_CH)
            key = jnp.sum(jnp.where(lane_e, key_ref[pl.ds(t0, MOE_CH), :], 0.0), axis=-1, keepdims=True)
            w = jnp.sum(jnp.where(lane_e, comb_ref[pl.ds(t0, MOE_CH), :], 0.0), axis=-1, keepdims=True)
            sel = jnp.where(key == pos, 1.0, 0.0).astype(BF16)
            o_ref[pl.ds(t0, MOE_CH), :] += w * jnp.dot(sel, y, preferred_element_type=F32)
            return carry2

        lax.fori_loop(c_lo, c_hi, scatter_chunk, 0)
        return carry

    lax.fori_loop(0, n_sub, expert_sub, 0)

    @pl.when((e == pl.num_programs(1) - 1) & (f == pl.num_programs(2) - 1))
    def _():
        o_ref[...] = _gate_residual(x_ref[...], o_ref[...], mod_ref[...])


def _ffn_moe(h, x, router_pad, w_gu_bf, w_down_bf, mods, m, l, seg, fc=1408):
    rows = x.shape[0]
    d = D_MODEL
    nf = D_FF // fc
    tt = MOE_TILE
    nt = rows // tt
    nch = tt // MOE_CH
    comb, key, keyt, cs = _moe_route(h, router_pad, m, tt)
    counts = cs.reshape(nt, MOE_CS_ROWS, LANES)[:, :nch + 1, :N_EXPERTS].reshape(-1)
    once = pl.Buffered(1)
    grid_spec = pltpu.PrefetchScalarGridSpec(
        num_scalar_prefetch=1,
        grid=(nt, N_EXPERTS, nf),
        in_specs=[
            pl.BlockSpec((tt, d), lambda i, e, f, cs: (i, 0), pipeline_mode=once),
            pl.BlockSpec((tt, d), lambda i, e, f, cs: (i, 0), pipeline_mode=once),
            pl.BlockSpec((tt, LANES), lambda i, e, f, cs: (i, 0), pipeline_mode=once),
            pl.BlockSpec((tt, LANES), lambda i, e, f, cs: (i, 0), pipeline_mode=once),
            pl.BlockSpec((N_EXPERTS, nch, 1, MOE_CH), lambda i, e, f, cs: (0, i, 0, 0), pipeline_mode=once),
            pl.BlockSpec((None, None, d, fc), lambda i, e, f, cs: (m, e, 0, f)),
            pl.BlockSpec((None, None, d, fc), lambda i, e, f, cs: (m, e, 0, nf + f)),
            pl.BlockSpec((None, None, fc, d), lambda i, e, f, cs: (m, e, f, 0)),
            pl.BlockSpec((None, SUBLANES, d), lambda i, e, f, cs: (l, seg, 5)),
        ],
        out_specs=pl.BlockSpec((tt, d), lambda i, e, f, cs: (i, 0), pipeline_mode=once),
        scratch_shapes=[pltpu.VMEM((tt + MOE_SB, d), BF16), pltpu.VMEM((MOE_SB, d), F32)],
    )
    return pl.pallas_call(
        functools.partial(_moe_kernel, tt=tt),
        out_shape=jax.ShapeDtypeStruct((rows, d), F32),
        grid_spec=grid_spec,
        compiler_params=_cparams(("arbitrary", "arbitrary", "arbitrary")),
        name="ffn_moe",
    )(counts, h, x, comb, key, keyt, w_gu_bf, w_gu_bf, w_down_bf, mods)


def _final_kernel(x_ref, g_ref, o_ref, *, batch):
    for b in range(batch):
        o_ref[b] = _rms(x_ref[:, b, :], g_ref[...])


def _final_norm(x, norm_final, batch, tl=64):
    rows, d = x.shape
    length = rows // batch
    return pl.pallas_call(
        functools.partial(_final_kernel, batch=batch),
        out_shape=jax.ShapeDtypeStruct((batch, length, d), F32),
        grid=(length // tl,),
        in_specs=[pl.BlockSpec((tl, batch, d), lambda i: (i, 0, 0)),
                  pl.BlockSpec((1, d), lambda i: (0, 0))],
        out_specs=pl.BlockSpec((batch, tl, d), lambda i: (0, i, 0)),
        compiler_params=_cparams(("arbitrary",)),
        name="final_norm",
    )(x.reshape(length, batch, d), norm_final.reshape(1, d))


def _time_major_kernel(x_ref, o_ref, *, batch):
    for b in range(batch):
        o_ref[:, b, :] = x_ref[b]


def _to_time_major(x, tl=64):
    batch, length, d = x.shape
    out = pl.pallas_call(
        functools.partial(_time_major_kernel, batch=batch),
        out_shape=jax.ShapeDtypeStruct((length, batch, d), x.dtype),
        grid=(length // tl,),
        in_specs=[pl.BlockSpec((batch, tl, d), lambda i: (0, i, 0))],
        out_specs=pl.BlockSpec((tl, batch, d), lambda i: (i, 0, 0)),
        compiler_params=_cparams(("arbitrary",)),
        name="to_time_major",
    )(x)
    return out.reshape(length * batch, d)


def _block_diag_heads(w):
    hh = N_LRU_HEADS // 2
    w = w.reshape(DEPTH, 2, 2, hh, LRU_HEAD_DIM, LRU_HEAD_DIM)
    eye = jnp.eye(hh, dtype=w.dtype)
    bd = jnp.einsum('ldgpij,pq->ldgpiqj', w, eye)
    return bd.reshape(DEPTH, 2, 2, hh * LRU_HEAD_DIM, hh * LRU_HEAD_DIM)


def kernel(x_prompt, x_sample, state_lru, c, c_ctx, norm_mix, norm_ffn, norm_final, ada_w, ada_b,
           w_in, w_out, hyena_short_w, hyena_short_b, filt_w1, filt_b1, filt_w2, filt_b2, filt_w3,
           filt_b3, filt_freq, hyena_bias, lru_conv_w, lru_conv_b, lru_wa, lru_ba, lru_wx, lru_bx,
           lru_lambda, ffn_w_gu, ffn_w_down, moe_router, moe_w_gu, moe_w_down):
    bp, lp, d = x_prompt.shape
    bs, ls, _ = x_sample.shape
    assert bs == SUBLANES and bp % SUBLANES == 0 and ls == (FFT_N1 // 2) * FFT_N2

    xs = _to_time_major(x_sample)
    xp = _to_time_major(x_prompt)

    cv = jnp.concatenate([c, jnp.broadcast_to(c_ctx[None, :], (SUBLANES, d))], axis=0)
    mods = _ada_mods(cv, ada_w, ada_b)

    w_in_bf = w_in.astype(BF16)
    w_out_bf = w_out.astype(BF16)
    ffn_gu_bf = ffn_w_gu.astype(BF16)
    ffn_down_bf = ffn_w_down.astype(BF16)
    moe_gu_bf = moe_w_gu.astype(BF16)
    moe_down_bf = moe_w_down.astype(BF16)
    router_pad = jnp.pad(moe_router, ((0, 0), (0, 0), (0, LANES - N_EXPERTS)))
    wbd = jnp.stack([_block_diag_heads(lru_wa), _block_diag_heads(lru_wx)], axis=2).astype(BF16)
    filt = dict(filt_w1=filt_w1, filt_b1=filt_b1, filt_w2=filt_w2, filt_b2=filt_b2,
                filt_w3=filt_w3, filt_b3=filt_b3, filt_freq=filt_freq)
    state_flat = state_lru.reshape(bs, DEPTH * 2 * D_LRU)
    zero_state = jnp.zeros((bp, D_LRU), F32)

    new_states = []
    for l in range(DEPTH):
        m = l // 2
        for path in range(2):
            if path == 0:
                x, batch, seg, grid_pos = xp, bp, 1, False
            else:
                x, batch, seg, grid_pos = xs, bs, 0, True
            length = x.shape[0] // batch

            proj = _in_proj(x, norm_mix, mods, w_in_bf, l, seg)
            u, x2, xr = _short_convs(proj, hyena_short_w, hyena_short_b, lru_conv_w, lru_conv_b, l, batch)

            ks, kd, ksum = _hyena_filter_taps(length, grid_pos, filt, l)
            if path == 0:
                khat = _filter_spectrum_direct(ks, kd, ksum)
                hy = _hyena_conv_direct(u, x2, khat, hyena_bias, l, batch)
            else:
                khat = _filter_spectrum_2stage(ks, kd, ksum)
                hy = _hyena_conv_2stage(u, x2, khat, hyena_bias, l, batch)

            lru_tm = min(2048, x.shape[0])
            if path == 0:
                h0f, h0f_blk, h0r, h0r_blk = zero_state, (0, 0), zero_state, (0, 0)
            else:
                h0f, h0f_blk, h0r, h0r_blk = state_flat, (0, 2 * l), state_flat, (0, 2 * l + 1)
            hf, hlast_f = _lru_scan(xr, wbd, lru_ba, lru_bx, lru_lambda, h0f, h0f_blk, l, 0, batch,
                                    tm=lru_tm)
            lr, hlast_r = _lru_scan(xr, wbd, lru_ba, lru_bx, lru_lambda, h0r, h0r_blk, l, 1, batch,
                                    hf=hf, proj=proj, tm=lru_tm)
            if path == 0:
                new_states.append(jnp.stack([hlast_f, hlast_r], axis=1))

            x, h = _out_proj(hy, lr, x, w_out_bf, norm_ffn, mods, l, seg)
            if l % 2 == 0:
                x = _ffn_dense(h, x, ffn_gu_bf, ffn_down_bf, mods, m, l, seg)
            else:
                x = _ffn_moe(h, x, router_pad, moe_gu_bf, moe_down_bf, mods, m, l, seg)
            if path == 0:
                xp = x
            else:
                xs = x

    new_state_lru = jnp.stack(new_states, axis=1).astype(state_lru.dtype)
    y_prompt = _final_norm(xp, norm_final, bp)
    y_sample = _final_norm(xs, norm_final, bs)
    return (y_prompt, y_sample, new_state_lru)
```

```python
import functools
import math

import numpy as np
import jax
import jax.numpy as jnp
from jax import lax
from jax.experimental import pallas as pl
from jax.experimental.pallas import tpu as pltpu

F32 = jnp.float32
BF16 = jnp.bfloat16

D_MODEL = 1024
DEPTH = 4
D_HYENA = 512
D_LRU = 512
N_IN = 3 * D_HYENA + 2 * D_LRU
N_LRU_HEADS = 8
LRU_HEAD_DIM = D_LRU // N_LRU_HEADS
LRU_C = 8.0
FILTER_HIDDEN = 64
N_BANDS = 4
POS_DIM = 1 + 2 * N_BANDS + 3
GRID_W = 64
HYENA_DECAY_FAST = 0.3
HYENA_DECAY_SLOW = 1.5
HYENA_DECAY_TARGET = 1e-2
HYENA_SHIFT = 0.05
D_FF = 2816
N_EXPERTS = 8
EPS = 1e-6

SUBLANES = 8
LANES = 128
VMEM_LIMIT = 56 * 1024 * 1024

FFT_N1 = 128
FFT_N2 = 64
K1_BLOCK = 8
N2_BLOCK = 8


def _cparams(sem):
    return pltpu.CompilerParams(dimension_semantics=sem, vmem_limit_bytes=VMEM_LIMIT)


def _split(a):
    hi = a.astype(BF16)
    lo = (a - hi.astype(F32)).astype(BF16)
    return hi, lo


def _dot(a, b, x3=False):
    if not x3:
        return jnp.dot(a.astype(BF16), b.astype(BF16), preferred_element_type=F32)
    ah, al = _split(a)
    bh, bl = _split(b)
    return (jnp.dot(ah, bh, preferred_element_type=F32)
            + jnp.dot(ah, bl, preferred_element_type=F32)
            + jnp.dot(al, bh, preferred_element_type=F32))


def _silu(x):
    return x * jax.nn.sigmoid(x)


def _gelu_tanh(x):
    return 0.5 * x * (1.0 + jnp.tanh(math.sqrt(2.0 / math.pi) * (x + 0.044715 * (x * x * x))))


def _rms(x, g):
    return x * lax.rsqrt(jnp.mean(x * x, axis=-1, keepdims=True) + EPS) * g


def _modulate(y, shift, scale):
    rows, d = y.shape
    y3 = y.reshape(rows // SUBLANES, SUBLANES, d)
    return (y3 * (1.0 + scale)[None] + shift[None]).reshape(rows, d)


@functools.lru_cache(maxsize=None)
def _dft_consts():
    n = FFT_N1 * FFT_N2
    half1 = FFT_N1 // 2
    k1 = np.arange(FFT_N1)[:, None]
    n1 = np.arange(half1)[None, :]
    ang = 2.0 * np.pi * k1 * n1 / FFT_N1
    c, s = np.cos(ang), np.sin(ang)
    m1 = np.block([[c, s], [-s, c]])
    k2 = np.arange(FFT_N2)[None, :, None]
    n2 = np.arange(FFT_N2)[None, None, :]
    kk1 = np.arange(FFT_N1)[:, None, None]
    th = 2.0 * np.pi * (n2 * k2 / FFT_N2 + n2 * kk1 / n)
    fr, fi = np.cos(th), -np.sin(th)
    g2 = np.concatenate([np.concatenate([fr, -fi], axis=2), np.concatenate([fi, fr], axis=2)], axis=1)
    g2t = np.transpose(g2, (0, 2, 1))
    cm, sm = c.T / n, s.T / n
    m3 = np.block([[cm, -sm], [sm, cm]])
    f32 = lambda a: np.asarray(a, np.float32)
    return f32(m1), f32(g2), f32(g2t), f32(m3)


@functools.lru_cache(maxsize=None)
def _inverse_stage1_interleaved(batch):
    _, _, _, m3 = _dft_consts()
    half1 = FFT_N1 // 2
    pairs = batch // 2
    big = np.zeros((half1, batch, pairs, 2 * FFT_N1), np.float32)
    for q in range(pairs):
        for part in range(2):
            big[:, 2 * q + part, q, :] = m3[part * half1:(part + 1) * half1]
    return big.reshape(half1 * batch, pairs * 2 * FFT_N1)


@functools.lru_cache(maxsize=None)
def _dft_consts_direct(length):
    n = 2 * length
    f = np.arange(n)[:, None]
    t = np.arange(length)[None, :]
    ang = 2.0 * np.pi * f * t / n
    c, s = np.cos(ang), np.sin(ang)
    fwd = np.block([[c, s], [-s, c]])
    ct, st = c.T / n, s.T / n
    inv = np.block([[ct, -st], [st, ct]])
    f32 = lambda a: np.asarray(a, np.float32)
    return f32(fwd), f32(inv), f32(c), f32(-s)


def _hyena_deltas():
    min_decay = math.log(HYENA_DECAY_TARGET) / HYENA_DECAY_SLOW
    max_decay = math.log(HYENA_DECAY_TARGET) / HYENA_DECAY_FAST
    return jnp.abs(jnp.linspace(min_decay, max_decay, D_HYENA, dtype=F32))


def _pos_feats(length, grid):
    d = jnp.arange(length, dtype=jnp.int32)
    t = d.astype(F32) / length
    bands = jnp.arange(1, N_BANDS + 1, dtype=F32)
    ang = 2.0 * math.pi * t[:, None] * bands[None, :]
    feats = [t[:, None], jnp.sin(ang), jnp.cos(ang)]
    if grid:
        rows = max(length // GRID_W, 1)
        col_ang = 2.0 * math.pi * (d % GRID_W).astype(F32) / GRID_W
        row = (d // GRID_W).astype(F32) / rows
        feats += [jnp.sin(col_ang)[:, None], jnp.cos(col_ang)[:, None], row[:, None]]
    else:
        feats.append(jnp.zeros((length, 3), F32))
    z = jnp.concatenate(feats, axis=-1)
    return jnp.pad(z, ((0, 0), (0, LANES - POS_DIM)))


def _ada_kernel(cv_ref, w_ref, b_ref, o_ref):
    s = _silu(cv_ref[...])
    o_ref[...] = _dot(s, w_ref[...]) + b_ref[...]


def _ada_mods(cv, ada_w, ada_b):
    tn = D_MODEL
    return pl.pallas_call(
        _ada_kernel,
        out_shape=jax.ShapeDtypeStruct((DEPTH, 16, 6 * D_MODEL), F32),
        grid=(DEPTH, 6 * D_MODEL // tn),
        in_specs=[
            pl.BlockSpec((16, D_MODEL), lambda l, j: (0, 0)),
            pl.BlockSpec((None, D_MODEL, tn), lambda l, j: (l, 0, j)),
            pl.BlockSpec((None, 1, tn), lambda l, j: (l, 0, j)),
        ],
        out_specs=pl.BlockSpec((None, 16, tn), lambda l, j: (l, 0, j)),
        compiler_params=_cparams(("arbitrary", "arbitrary")),
        name="ada_mods",
    )(cv, ada_w, ada_b.reshape(DEPTH, 1, 6 * D_MODEL))


def _in_kernel(x_ref, prev_ref, next_ref, g_ref, mod_ref, w_ref, sw_ref, sb_ref, cw_ref, cb_ref,
               u_ref, x2_ref, xr_ref, pg_ref, *, batch, tm):
    i = pl.program_id(0)
    last = pl.num_programs(0) - 1
    lead = 2 * batch
    xe = jnp.concatenate([prev_ref[...], x_ref[...], next_ref[...]], axis=0)
    h = _modulate(_rms(xe, g_ref[...]), mod_ref[:, 0:D_MODEL], mod_ref[:, D_MODEL:2 * D_MODEL])
    h = jnp.concatenate([h[:lead] * (i > 0).astype(F32), h[lead:lead + tm],
                         h[lead + tm:] * (i < last).astype(F32)], axis=0)
    p = jnp.dot(h.astype(BF16), w_ref[...], preferred_element_type=F32)

    def tap(d, c0, c1):
        return p[lead + d * batch:lead + d * batch + tm, c0:c1]

    def hyena_branch(j):
        c0, c1 = j * D_HYENA, (j + 1) * D_HYENA
        acc = sb_ref[:, c0:c1] + tap(-1, c0, c1) * sw_ref[0:1, c0:c1]
        acc = acc + tap(0, c0, c1) * sw_ref[1:2, c0:c1]
        return acc + tap(1, c0, c1) * sw_ref[2:3, c0:c1]

    u_ref[...] = hyena_branch(0) * hyena_branch(1)
    x2_ref[...] = hyena_branch(2)
    c0, c1 = 3 * D_HYENA, 3 * D_HYENA + D_LRU
    acc = cb_ref[...] + tap(-2, c0, c1) * cw_ref[0:1, :]
    acc = acc + tap(-1, c0, c1) * cw_ref[1:2, :]
    acc = acc + tap(0, c0, c1) * cw_ref[2:3, :]
    xr_ref[...] = acc + tap(1, c0, c1) * cw_ref[3:4, :]
    pg_ref[...] = tap(0, c1, c1 + D_LRU)


def _in_proj_convs(x, norm_mix, mods, w_in_bf, short_w, short_b, conv_w, conv_b, l, seg, batch, tm=512):
    rows = x.shape[0]
    per_prev = tm // (2 * batch)
    per_next = tm // batch
    n_next = rows // batch
    out = jax.ShapeDtypeStruct((rows, D_HYENA), F32)
    return pl.pallas_call(
        functools.partial(_in_kernel, batch=batch, tm=tm),
        out_shape=(out, out, out, out),
        grid=(rows // tm,),
        in_specs=[
            pl.BlockSpec((tm, D_MODEL), lambda i: (i, 0)),
            pl.BlockSpec((2 * batch, D_MODEL), lambda i: (jnp.maximum(i * per_prev - 1, 0), 0)),
            pl.BlockSpec((batch, D_MODEL), lambda i: (jnp.minimum((i + 1) * per_next, n_next - 1), 0)),
            pl.BlockSpec((None, 1, D_MODEL), lambda i: (l, 0, 0)),
            pl.BlockSpec((None, SUBLANES, 2 * D_MODEL), lambda i: (l, seg, 0)),
            pl.BlockSpec((None, D_MODEL, N_IN), lambda i: (l, 0, 0)),
            pl.BlockSpec((None, 3, 3 * D_HYENA), lambda i: (l, 0, 0)),
            pl.BlockSpec((None, 1, 3 * D_HYENA), lambda i: (l, 0, 0)),
            pl.BlockSpec((None, 4, D_LRU), lambda i: (l, 0, 0)),
            pl.BlockSpec((None, 1, D_LRU), lambda i: (l, 0, 0)),
        ],
        out_specs=(pl.BlockSpec((tm, D_HYENA), lambda i: (i, 0)),) * 4,
        compiler_params=_cparams(("arbitrary",)),
        name="in_proj_convs",
    )(x, x, x, norm_mix.reshape(DEPTH, 1, D_MODEL), mods, w_in_bf,
      short_w, short_b.reshape(DEPTH, 1, 3 * D_HYENA), conv_w, conv_b.reshape(DEPTH, 1, D_LRU))


def _lru_gates(xr, w_ref, ba, bx, sp):
    xb = xr.astype(BF16)
    half = D_LRU // 2
    ra, rx = [], []
    for hf in range(2):
        xs = xb[:, hf * half:(hf + 1) * half]
        ra.append(jnp.dot(xs, w_ref[0, hf], preferred_element_type=F32))
        rx.append(jnp.dot(xs, w_ref[1, hf], preferred_element_type=F32))
    gate_a = 0.5 + 0.5 * jnp.tanh(0.5 * (jnp.concatenate(ra, axis=1) + ba))
    gate_x = 0.5 + 0.5 * jnp.tanh(0.5 * (jnp.concatenate(rx, axis=1) + bx))
    log_a = (-LRU_C) * gate_a * sp
    a = jnp.exp(log_a)
    mult = jnp.sqrt(jnp.maximum(jnp.tanh(-log_a) * (1.0 + a * a), 0.0))
    return a, xr * gate_x * mult


def _softplus(x):
    return jnp.maximum(x, 0.0) + jnp.log1p(jnp.exp(-jnp.abs(x)))


def _lru_kernel(*refs, batch, tm, chunk, reverse):
    if reverse:
        (xr_ref, w_ref, ba_ref, bx_ref, lam_ref, h0_ref, hf_ref, pg_ref,
         y_ref, hl_ref, a_scr, b_scr, h_scr) = refs
    else:
        (xr_ref, w_ref, ba_ref, bx_ref, lam_ref, h0_ref,
         y_ref, hl_ref, a_scr, b_scr, h_scr) = refs
    i = pl.program_id(0)

    @pl.when(i == 0)
    def _():
        h_scr[...] = h0_ref[...].astype(F32)

    sp = _softplus(-lam_ref[...])
    ba = ba_ref[...]
    bx = bx_ref[...]

    def fill(c, carry):
        r = pl.multiple_of(c * chunk, chunk)
        a, b = _lru_gates(xr_ref[pl.ds(r, chunk), :], w_ref, ba, bx, sp)
        a_scr[pl.ds(r, chunk), :] = a
        b_scr[pl.ds(r, chunk), :] = b
        return carry

    lax.fori_loop(0, tm // chunk, fill, 0)

    steps = tm // batch

    def step(s, h):
        t = (steps - 1 - s) if reverse else s
        r = pl.multiple_of(t * batch, batch)
        h = a_scr[pl.ds(r, batch), :] * h + b_scr[pl.ds(r, batch), :]
        if reverse:
            y_ref[pl.ds(r, batch), :] = ((hf_ref[pl.ds(r, batch), :] + h)
                                         * _gelu_tanh(pg_ref[pl.ds(r, batch), :]))
        else:
            y_ref[pl.ds(r, batch), :] = h
        return h

    h = lax.fori_loop(0, steps, step, h_scr[...], unroll=8)
    h_scr[...] = h

    @pl.when(i == pl.num_programs(0) - 1)
    def _():
        hl_ref[...] = h


def _lru_scan(xr, wbd, lru_ba, lru_bx, lru_lambda, h0, h0_block, l, direction, batch,
              hf=None, pg=None, tm=2048, chunk=256):
    rows = xr.shape[0]
    nt = rows // tm
    reverse = direction == 1
    row_map = (lambda i: (nt - 1 - i, 0)) if reverse else (lambda i: (i, 0))
    vec = lambda: pl.BlockSpec((None, None, 1, D_LRU), lambda i: (l, direction, 0, 0))
    in_specs = [
        pl.BlockSpec((tm, D_LRU), row_map),
        pl.BlockSpec((None, None, 2, 2, D_LRU // 2, D_LRU // 2), lambda i: (l, direction, 0, 0, 0, 0)),
        vec(), vec(), vec(),
        pl.BlockSpec((batch, D_LRU), lambda i: h0_block),
    ]
    args = [xr, wbd, lru_ba.reshape(DEPTH, 2, 1, D_LRU), lru_bx.reshape(DEPTH, 2, 1, D_LRU),
            lru_lambda.reshape(DEPTH, 2, 1, D_LRU), h0]
    if reverse:
        in_specs += [pl.BlockSpec((tm, D_LRU), row_map), pl.BlockSpec((tm, D_LRU), row_map)]
        args += [hf, pg]
    return pl.pallas_call(
        functools.partial(_lru_kernel, batch=batch, tm=tm, chunk=chunk, reverse=reverse),
        out_shape=(jax.ShapeDtypeStruct((rows, D_LRU), F32), jax.ShapeDtypeStruct((batch, D_LRU), F32)),
        grid=(nt,),
        in_specs=in_specs,
        out_specs=(pl.BlockSpec((tm, D_LRU), row_map), pl.BlockSpec((batch, D_LRU), lambda i: (0, 0))),
        scratch_shapes=[pltpu.VMEM((tm, D_LRU), F32), pltpu.VMEM((tm, D_LRU), F32),
                        pltpu.VMEM((batch, D_LRU), F32)],
        compiler_params=_cparams(("arbitrary",)),
        name="lru_rev" if reverse else "lru_fwd",
    )(*args)


def _filter_kernel(z_ref, w1_ref, b1_ref, w2_ref, b2_ref, w3_ref, b3_ref, fr_ref, dl_ref,
                   ks_ref, kd_ref, sum_ref):
    i = pl.program_id(0)
    hi = lax.Precision.HIGHEST
    z = z_ref[...]
    h1 = jnp.sin(fr_ref[0:1, :] * (jnp.dot(z, w1_ref[...], precision=hi, preferred_element_type=F32)
                                   + b1_ref[...]))
    h2 = jnp.sin(fr_ref[1:2, :] * (jnp.dot(h1, w2_ref[...], precision=hi, preferred_element_type=F32)
                                   + b2_ref[...]))
    k = jnp.dot(h2, w3_ref[...], precision=hi, preferred_element_type=F32) + b3_ref[...]
    window = jnp.exp(-z[:, 0:1] * dl_ref[...]) + HYENA_SHIFT
    kf = k[:, :D_HYENA] * window
    kb = k[:, D_HYENA:] * window
    row = lax.broadcasted_iota(jnp.int32, kb.shape, 0) + i * kb.shape[0]
    kb = jnp.where(row == 0, 0.0, kb)
    ks_ref[...] = kf + kb
    kd_ref[...] = kf - kb
    part = jnp.sum(jnp.abs(kf) + jnp.abs(kb), axis=0, keepdims=True)

    @pl.when(i == 0)
    def _():
        sum_ref[...] = part

    @pl.when(i > 0)
    def _():
        sum_ref[...] += part


def _hyena_filter_taps(length, grid, p, l, tl=256, n2_major=False):
    z = _pos_feats(length, grid)
    if n2_major:
        z = z.reshape(length // FFT_N2, FFT_N2, LANES).transpose(1, 0, 2).reshape(length, LANES)
    w1 = jnp.pad(p['filt_w1'], ((0, 0), (0, LANES - POS_DIM), (0, 0)))
    tap = jax.ShapeDtypeStruct((length, D_HYENA), F32)
    h = FILTER_HIDDEN
    return pl.pallas_call(
        _filter_kernel,
        out_shape=(tap, tap, jax.ShapeDtypeStruct((1, D_HYENA), F32)),
        grid=(length // tl,),
        in_specs=[
            pl.BlockSpec((tl, LANES), lambda i: (i, 0)),
            pl.BlockSpec((None, LANES, h), lambda i: (l, 0, 0)),
            pl.BlockSpec((None, 1, h), lambda i: (l, 0, 0)),
            pl.BlockSpec((None, h, h), lambda i: (l, 0, 0)),
            pl.BlockSpec((None, 1, h), lambda i: (l, 0, 0)),
            pl.BlockSpec((None, h, 2 * D_HYENA), lambda i: (l, 0, 0)),
            pl.BlockSpec((None, 1, 2 * D_HYENA), lambda i: (l, 0, 0)),
            pl.BlockSpec((None, 2, h), lambda i: (l, 0, 0)),
            pl.BlockSpec((1, D_HYENA), lambda i: (0, 0)),
        ],
        out_specs=(pl.BlockSpec((tl, D_HYENA), lambda i: (i, 0)),
                   pl.BlockSpec((tl, D_HYENA), lambda i: (i, 0)),
                   pl.BlockSpec((1, D_HYENA), lambda i: (0, 0))),
        compiler_params=_cparams(("arbitrary",)),
        name="hyena_filter",
    )(z, w1, p['filt_b1'].reshape(DEPTH, 1, h), p['filt_w2'], p['filt_b2'].reshape(DEPTH, 1, h),
      p['filt_w3'], p['filt_b3'].reshape(DEPTH, 1, 2 * D_HYENA), p['filt_freq'],
      _hyena_deltas().reshape(1, D_HYENA))


def _spec_direct_kernel(c_ref, s_ref, ks_ref, kd_ref, sum_ref, o_ref):
    scale = 1.0 / (sum_ref[...] + EPS)
    o_ref[0] = _dot(c_ref[...], ks_ref[...], x3=True) * scale
    o_ref[1] = _dot(s_ref[...], kd_ref[...], x3=True) * scale


def _filter_spectrum_direct(ks, kd, ksum):
    length = ks.shape[0]
    n = 2 * length
    _, _, c, ms = _dft_consts_direct(length)
    return pl.pallas_call(
        _spec_direct_kernel,
        out_shape=jax.ShapeDtypeStruct((2, n, D_HYENA), F32),
        compiler_params=_cparams(()),
        name="filter_spectrum_direct",
    )(jnp.asarray(c), jnp.asarray(ms), ks, kd, ksum)


def _conv_direct_kernel(u_ref, x2_ref, kh_ref, fwd_ref, inv_ref, bias_ref, o_ref, *, length, batch):
    n = 2 * length
    kr, ki = kh_ref[0], kh_ref[1]
    for q in range(batch // 2):
        z = jnp.concatenate([u_ref[:, 2 * q, :], u_ref[:, 2 * q + 1, :]], axis=0)
        x = _dot(fwd_ref[...], z, x3=True)
        xr, xi = x[:n], x[n:]
        y = jnp.concatenate([xr * kr - xi * ki, xr * ki + xi * kr], axis=0)
        conv = _dot(inv_ref[...], y, x3=True)
        o_ref[:, 2 * q, :] = conv[:length]
        o_ref[:, 2 * q + 1, :] = conv[length:]
    o_ref[...] = x2_ref[...] * (o_ref[...] + u_ref[...] * bias_ref[...])


def _hyena_conv_direct(u, x2, khat, hyena_bias, l, batch, cw=256):
    rows = u.shape[0]
    length = rows // batch
    fwd, inv, _, _ = _dft_consts_direct(length)
    blk = pl.BlockSpec((length, batch, cw), lambda j: (0, 0, j))
    out = pl.pallas_call(
        functools.partial(_conv_direct_kernel, length=length, batch=batch),
        out_shape=jax.ShapeDtypeStruct((length, batch, D_HYENA), F32),
        grid=(D_HYENA // cw,),
        in_specs=[blk, blk,
                  pl.BlockSpec((2, 2 * length, cw), lambda j: (0, 0, j)),
                  pl.BlockSpec(fwd.shape, lambda j: (0, 0)),
                  pl.BlockSpec(inv.shape, lambda j: (0, 0)),
                  pl.BlockSpec((None, 1, cw), lambda j: (l, 0, j))],
        out_specs=blk,
        compiler_params=_cparams(("arbitrary",)),
        name="hyena_conv_direct",
    )(u.reshape(length, batch, D_HYENA), x2.reshape(length, batch, D_HYENA), khat,
      jnp.asarray(fwd), jnp.asarray(inv), hyena_bias.reshape(DEPTH, 1, D_HYENA))
    return out.reshape(rows, D_HYENA)


def _stage1_kernel(x_ref, m_ref, o_ref, *, batch):
    m = m_ref[...].astype(BF16)
    for j in range(N2_BLOCK):
        for q in range(batch // 2):
            z = jnp.concatenate([x_ref[:, j, 2 * q, :], x_ref[:, j, 2 * q + 1, :]], axis=0)
            o_ref[j, q] = jnp.dot(m, z.astype(BF16), preferred_element_type=F32)


def _fft_stage1(u, batch, cw=256):
    c = u.shape[1]
    half1 = FFT_N1 // 2
    pairs = batch // 2
    m1, _, _, _ = _dft_consts()
    return pl.pallas_call(
        functools.partial(_stage1_kernel, batch=batch),
        out_shape=jax.ShapeDtypeStruct((FFT_N2, pairs, 2 * FFT_N1, c), F32),
        grid=(FFT_N2 // N2_BLOCK, c // cw),
        in_specs=[pl.BlockSpec((half1, N2_BLOCK, batch, cw), lambda j, k: (0, j, 0, k)),
                  pl.BlockSpec(m1.shape, lambda j, k: (0, 0))],
        out_specs=pl.BlockSpec((N2_BLOCK, pairs, 2 * FFT_N1, cw), lambda j, k: (j, 0, 0, k)),
        compiler_params=_cparams(("arbitrary", "arbitrary")),
        name="fft_stage1",
    )(u.reshape(half1, FFT_N2, batch, c), jnp.asarray(m1))


def _stage1_filter_kernel(ks_ref, kd_ref, m_ref, o_ref):
    m = m_ref[:, :FFT_N1 // 2]
    for j in range(N2_BLOCK):
        for g, ref in enumerate((ks_ref, kd_ref)):
            o_ref[j, g] = _dot(m, ref[j], x3=True)


def _stage2_filter_kernel(a_ref, g_ref, sum_ref, o_ref):
    scale = 1.0 / (sum_ref[...] + EPS)
    for j in range(K1_BLOCK):
        bs = jnp.concatenate([a_ref[:, 0, 0, j, :], a_ref[:, 0, 1, j, :]], axis=0)
        bd = jnp.concatenate([a_ref[:, 1, 0, j, :], a_ref[:, 1, 1, j, :]], axis=0)
        o_ref[0, j] = _dot(g_ref[j], bs, x3=True)[:FFT_N2] * scale
        o_ref[1, j] = _dot(g_ref[j], bd, x3=True)[FFT_N2:] * scale


def _filter_spectrum_2stage(ks, kd, ksum):
    c = D_HYENA
    half1 = FFT_N1 // 2
    m1, g2, _, _ = _dft_consts()
    tap = pl.BlockSpec((N2_BLOCK, half1, c), lambda j: (j, 0, 0))
    a = pl.pallas_call(
        _stage1_filter_kernel,
        out_shape=jax.ShapeDtypeStruct((FFT_N2, 2, 2 * FFT_N1, c), F32),
        grid=(FFT_N2 // N2_BLOCK,),
        in_specs=[tap, tap, pl.BlockSpec(m1.shape, lambda j: (0, 0))],
        out_specs=pl.BlockSpec((N2_BLOCK, 2, 2 * FFT_N1, c), lambda j: (j, 0, 0, 0)),
        compiler_params=_cparams(("arbitrary",)),
        name="filter_spectrum_stage1",
    )(ks.reshape(FFT_N2, half1, c), kd.reshape(FFT_N2, half1, c), jnp.asarray(m1))
    return pl.pallas_call(
        _stage2_filter_kernel,
        out_shape=jax.ShapeDtypeStruct((2, FFT_N1, FFT_N2, c), F32),
        grid=(FFT_N1 // K1_BLOCK,),
        in_specs=[pl.BlockSpec((FFT_N2, 2, 2, K1_BLOCK, c), lambda i: (0, 0, 0, i, 0)),
                  pl.BlockSpec((K1_BLOCK, 2 * FFT_N2, 2 * FFT_N2), lambda i: (i, 0, 0)),
                  pl.BlockSpec((1, c), lambda i: (0, 0))],
        out_specs=pl.BlockSpec((2, K1_BLOCK, FFT_N2, c), lambda i: (0, i, 0, 0)),
        compiler_params=_cparams(("arbitrary",)),
        name="filter_spectrum_stage2",
    )(a.reshape(FFT_N2, 2, 2, FFT_N1, c), jnp.asarray(g2), ksum)


def _stage2_kernel(a_ref, kh_ref, g_ref, gt_ref, o_ref):
    for j in range(K1_BLOCK):
        b = jnp.concatenate([a_ref[:, 0, j, :], a_ref[:, 1, j, :]], axis=0)
        x = _dot(g_ref[j], b)
        xr, xi = x[:FFT_N2], x[FFT_N2:]
        kr, ki = kh_ref[0, j], kh_ref[1, j]
        y = jnp.concatenate([xr * kr - xi * ki, xr * ki + xi * kr], axis=0)
        o_ref[j] = _dot(gt_ref[j], y)


def _fft_stage2(a, khat):
    _, g2, g2t, _ = _dft_consts()
    c = D_HYENA
    pairs = a.shape[1]
    gblk = pl.BlockSpec((K1_BLOCK, 2 * FFT_N2, 2 * FFT_N2), lambda i, q: (i, 0, 0))
    return pl.pallas_call(
        _stage2_kernel,
        out_shape=jax.ShapeDtypeStruct((FFT_N1, pairs, 2 * FFT_N2, c), F32),
        grid=(FFT_N1 // K1_BLOCK, pairs),
        in_specs=[pl.BlockSpec((FFT_N2, None, 2, K1_BLOCK, c), lambda i, q: (0, q, 0, i, 0)),
                  pl.BlockSpec((2, K1_BLOCK, FFT_N2, c), lambda i, q: (0, i, 0, 0)), gblk, gblk],
        out_specs=pl.BlockSpec((K1_BLOCK, None, 2 * FFT_N2, c), lambda i, q: (i, q, 0, 0)),
        compiler_params=_cparams(("arbitrary", "arbitrary")),
        name="fft_stage2",
    )(a.reshape(FFT_N2, pairs, 2, FFT_N1, c), khat, jnp.asarray(g2), jnp.asarray(g2t))


def _stage3_kernel(a_ref, m_ref, u_ref, x2_ref, bias_ref, o_ref, *, batch):
    half1 = FFT_N1 // 2
    m = m_ref[...].astype(BF16)
    for j in range(N2_BLOCK):
        b = jnp.concatenate([a_ref[:, q, ri, j, :] for q in range(batch // 2) for ri in range(2)], axis=0)
        y = jnp.dot(m, b.astype(BF16), preferred_element_type=F32)
        o_ref[:, j] = y.reshape(half1, batch, y.shape[-1])
    o_ref[...] = x2_ref[...] * (o_ref[...] + u_ref[...] * bias_ref[...])


def _fft_stage3(a, u, x2, hyena_bias, l, batch, cw=256):
    rows, c = u.shape
    half1 = FFT_N1 // 2
    pairs = batch // 2
    m3 = _inverse_stage1_interleaved(batch)
    blk = pl.BlockSpec((half1, N2_BLOCK, batch, cw), lambda j, k: (0, j, 0, k))
    out = pl.pallas_call(
        functools.partial(_stage3_kernel, batch=batch),
        out_shape=jax.ShapeDtypeStruct((half1, FFT_N2, batch, c), F32),
        grid=(FFT_N2 // N2_BLOCK, c // cw),
        in_specs=[pl.BlockSpec((FFT_N1, pairs, 2, N2_BLOCK, cw), lambda j, k: (0, 0, 0, j, k)),
                  pl.BlockSpec(m3.shape, lambda j, k: (0, 0)),
                  blk, blk,
                  pl.BlockSpec((None, 1, cw), lambda j, k: (l, 0, k))],
        out_specs=blk,
        compiler_params=_cparams(("arbitrary", "arbitrary")),
        name="fft_stage3",
    )(a.reshape(FFT_N1, pairs, 2, FFT_N2, c), jnp.asarray(m3), u.reshape(half1, FFT_N2, batch, c),
      x2.reshape(half1, FFT_N2, batch, c), hyena_bias.reshape(DEPTH, 1, c))
    return out.reshape(rows, c)


def _hyena_conv_2stage(u, x2, khat, hyena_bias, l, batch):
    a = _fft_stage1(u, batch)
    a = _fft_stage2(a, khat)
    return _fft_stage3(a, u, x2, hyena_bias, l, batch)


def _out_kernel(hy_ref, lr_ref, x_ref, w_ref, g_ref, mod_ref, xo_ref, h_ref):
    d = D_MODEL
    mixed = (jnp.dot(hy_ref[...].astype(BF16), w_ref[0:D_HYENA, :], preferred_element_type=F32)
             + jnp.dot(lr_ref[...].astype(BF16), w_ref[D_HYENA:, :], preferred_element_type=F32))
    rows = mixed.shape[0]
    g1 = mod_ref[:, 2 * d:3 * d]
    x = x_ref[...] + (mixed.reshape(rows // SUBLANES, SUBLANES, d) * g1[None]).reshape(rows, d)
    xo_ref[...] = x
    h = _modulate(_rms(x, g_ref[...]), mod_ref[:, 3 * d:4 * d], mod_ref[:, 4 * d:5 * d])
    h_ref[...] = h.astype(BF16)


def _out_proj(hy, lr, x, w_out_bf, norm_ffn, mods, l, seg, tm=512):
    rows = x.shape[0]
    d = D_MODEL
    return pl.pallas_call(
        _out_kernel,
        out_shape=(jax.ShapeDtypeStruct((rows, d), F32), jax.ShapeDtypeStruct((rows, d), BF16)),
        grid=(rows // tm,),
        in_specs=[
            pl.BlockSpec((tm, D_HYENA), lambda i: (i, 0)),
            pl.BlockSpec((tm, D_LRU), lambda i: (i, 0)),
            pl.BlockSpec((tm, d), lambda i: (i, 0)),
            pl.BlockSpec((None, D_HYENA + D_LRU, d), lambda i: (l, 0, 0)),
            pl.BlockSpec((None, 1, d), lambda i: (l, 0, 0)),
            pl.BlockSpec((None, SUBLANES, 6 * d), lambda i: (l, seg, 0)),
        ],
        out_specs=(pl.BlockSpec((tm, d), lambda i: (i, 0)), pl.BlockSpec((tm, d), lambda i: (i, 0))),
        compiler_params=_cparams(("arbitrary",)),
        name="out_proj",
    )(hy, lr, x, w_out_bf, norm_ffn.reshape(DEPTH, 1, d), mods)


def _gate_residual(x, acc, g2):
    rows, d = acc.shape
    return x + (acc.reshape(rows // SUBLANES, SUBLANES, d) * g2[None]).reshape(rows, d)


def _ffn_kernel(h_ref, x_ref, wg_ref, wu_ref, wd_ref, mod_ref, o_ref, acc_ref):
    f = pl.program_id(1)
    h = h_ref[...]
    g = jnp.dot(h, wg_ref[...], preferred_element_type=F32)
    u = jnp.dot(h, wu_ref[...], preferred_element_type=F32)
    part = jnp.dot((_silu(g) * u).astype(BF16), wd_ref[...], preferred_element_type=F32)

    @pl.when(f == 0)
    def _():
        acc_ref[...] = part

    @pl.when(f > 0)
    def _():
        acc_ref[...] += part

    @pl.when(f == pl.num_programs(1) - 1)
    def _():
        o_ref[...] = _gate_residual(x_ref[...], acc_ref[...], mod_ref[...])


def _ffn_dense(h, x, w_gu_bf, w_down_bf, mods, m, l, seg, tm=512, fc=1408):
    rows = x.shape[0]
    d = D_MODEL
    nf = D_FF // fc
    return pl.pallas_call(
        _ffn_kernel,
        out_shape=jax.ShapeDtypeStruct((rows, d), F32),
        grid=(rows // tm, nf),
        in_specs=[
            pl.BlockSpec((tm, d), lambda i, f: (i, 0)),
            pl.BlockSpec((tm, d), lambda i, f: (i, 0)),
            pl.BlockSpec((None, d, fc), lambda i, f: (m, 0, f)),
            pl.BlockSpec((None, d, fc), lambda i, f: (m, 0, nf + f)),
            pl.BlockSpec((None, fc, d), lambda i, f: (m, f, 0)),
            pl.BlockSpec((None, SUBLANES, d), lambda i, f: (l, seg, 5)),
        ],
        out_specs=pl.BlockSpec((tm, d), lambda i, f: (i, 0)),
        scratch_shapes=[pltpu.VMEM((tm, d), F32)],
        compiler_params=_cparams(("arbitrary", "arbitrary")),
        name="ffn_dense",
    )(h, x, w_gu_bf, w_gu_bf, w_down_bf, mods)


def _route_top2(logits):
    lane = lax.broadcasted_iota(jnp.int32, logits.shape, 1)
    neg = jnp.float32(-jnp.inf)
    v = jnp.where(lane < N_EXPERTS, logits, neg)
    m1 = jnp.max(v, axis=-1, keepdims=True)
    i1 = jnp.min(jnp.where(v == m1, lane, LANES), axis=-1, keepdims=True)
    v2 = jnp.where(lane == i1, neg, v)
    m2 = jnp.max(v2, axis=-1, keepdims=True)
    i2 = jnp.min(jnp.where(v2 == m2, lane, LANES), axis=-1, keepdims=True)
    e2 = jnp.exp(m2 - m1)
    w1 = 1.0 / (1.0 + e2)
    w2 = e2 / (1.0 + e2)
    first = lane == i1
    second = lane == i2
    comb = jnp.where(first, w1, 0.0) + jnp.where(second, w2, 0.0)
    return comb, jnp.where(first | second, 1.0, 0.0)


MOE_TILE = 2048
MOE_CH = 256
MOE_SB = 128
MOE_WIN = 256
MOE_CS_ROWS = 16


def _router_kernel(h_ref, r_ref, comb_ref, key_ref, keyt_ref, cs_ref, *, tt):
    nch = tt // MOE_CH
    logits = _dot(h_ref[...].astype(F32), r_ref[...], x3=True)
    comb, sel = _route_top2(logits)
    comb_ref[...] = comb
    rr = lax.broadcasted_iota(jnp.int32, (MOE_CH, MOE_CH), 0)
    cc = lax.broadcasted_iota(jnp.int32, (MOE_CH, MOE_CH), 1)
    strict_lower = jnp.where(cc < rr, 1.0, 0.0).astype(BF16)
    cs_ref[...] = jnp.zeros(cs_ref.shape, jnp.int32)
    carry = jnp.zeros((1, LANES), F32)
    for c in range(nch):
        selc = sel[c * MOE_CH:(c + 1) * MOE_CH]
        cs_ref[c:c + 1, :] = carry.astype(jnp.int32)
        rank = jnp.dot(strict_lower, selc.astype(BF16), preferred_element_type=F32) + carry
        key_ref[c * MOE_CH:(c + 1) * MOE_CH, :] = jnp.where(selc > 0.0, rank, -1.0)
        carry = carry + jnp.sum(selc, axis=0, keepdims=True)
    cs_ref[nch:nch + 1, :] = carry.astype(jnp.int32)
    keyt = jnp.transpose(key_ref[...])
    for e in range(N_EXPERTS):
        for c in range(nch):
            keyt_ref[e, c] = keyt[e:e + 1, c * MOE_CH:(c + 1) * MOE_CH]


def _moe_route(h, router_pad, m, tt):
    rows, d = h.shape
    nt = rows // tt
    nch = tt // MOE_CH
    tok = jax.ShapeDtypeStruct((rows, LANES), F32)
    return pl.pallas_call(
        functools.partial(_router_kernel, tt=tt),
        out_shape=(tok, tok,
                   jax.ShapeDtypeStruct((N_EXPERTS, rows // MOE_CH, 1, MOE_CH), F32),
                   jax.ShapeDtypeStruct((nt * MOE_CS_ROWS, LANES), jnp.int32)),
        grid=(nt,),
        in_specs=[pl.BlockSpec((tt, d), lambda i: (i, 0)),
                  pl.BlockSpec((None, d, LANES), lambda i: (m, 0, 0))],
        out_specs=(pl.BlockSpec((tt, LANES), lambda i: (i, 0)),
                   pl.BlockSpec((tt, LANES), lambda i: (i, 0)),
                   pl.BlockSpec((N_EXPERTS, nch, 1, MOE_CH), lambda i: (0, i, 0, 0)),
                   pl.BlockSpec((MOE_CS_ROWS, LANES), lambda i: (i, 0))),
        compiler_params=_cparams(("arbitrary",)),
        name="moe_route",
    )(h, router_pad)


def _moe_kernel(cs_ref, h_ref, x_ref, comb_ref, key_ref, keyt_ref, wg_ref, wu_ref, wd_ref, mod_ref,
                o_ref, xg_ref, y_ref, *, tt):
    i = pl.program_id(0)
    e = pl.program_id(1)
    f = pl.program_id(2)
    last_f = pl.num_programs(2) - 1
    nch = tt // MOE_CH
    base = i * ((nch + 1) * N_EXPERTS) + e

    def count_before(c):
        return cs_ref[base + c * N_EXPERTS]

    n_sub = (count_before(nch) + (MOE_SB - 1)) // MOE_SB

    def windows(c):
        lo = count_before(c)
        hi = count_before(c + 1)
        w0 = (lo // MOE_SB) * MOE_SB
        return w0, (hi - w0 + (MOE_WIN - 1)) // MOE_WIN

    def row_block(r0, size):
        return pl.ds(pl.multiple_of(r0, MOE_SB), size)

    @pl.when((e == 0) & (f == 0))
    def _():
        o_ref[...] = jnp.zeros_like(o_ref)

    @pl.when(f == 0)
    def _():
        def clear(j, carry):
            xg_ref[row_block(j * MOE_SB, MOE_SB), :] = jnp.zeros((MOE_SB, D_MODEL), BF16)
            return carry

        lax.fori_loop(0, n_sub + MOE_WIN // MOE_SB, clear, 0)
        y_ref[row_block(n_sub * MOE_SB, MOE_WIN), :] = jnp.zeros((MOE_WIN, D_MODEL), BF16)

        def gather(c, r0):
            row = (lax.broadcasted_iota(jnp.int32, (MOE_WIN, MOE_CH), 0) + r0).astype(F32)
            sel = jnp.where(keyt_ref[e, c] == row, 1.0, 0.0).astype(BF16)
            part = jnp.dot(sel, h_ref[c * MOE_CH:(c + 1) * MOE_CH, :], preferred_element_type=F32)
            xg_ref[row_block(r0, MOE_WIN), :] += part.astype(BF16)

        starts = [windows(c) for c in range(nch)]
        for c in range(nch):
            gather(c, starts[c][0])
        for c in range(nch):
            w0, n_win = starts[c]
            lax.fori_loop(1, n_win, lambda k, carry, c=c, w0=w0: (gather(c, w0 + k * MOE_WIN), carry)[1], 0)

    def expert_sub(j, carry):
        rows = row_block(j * MOE_SB, MOE_SB)
        xg = xg_ref[rows, :]
        g = jnp.dot(xg, wg_ref[...], preferred_element_type=F32)
        u = jnp.dot(xg, wu_ref[...], preferred_element_type=F32)
        y = jnp.dot((_silu(g) * u).astype(BF16), wd_ref[...], preferred_element_type=F32)

        @pl.when(f == 0)
        def _():
            y_ref[rows, :] = y.astype(BF16)

        @pl.when(f > 0)
        def _():
            y_ref[rows, :] = (y_ref[rows, :].astype(F32) + y).astype(BF16)

        return carry

    lax.fori_loop(0, n_sub, expert_sub, 0)

    @pl.when(f == last_f)
    def _():
        lane_e = lax.broadcasted_iota(jnp.int32, (MOE_CH, LANES), 1) == e

        def scatter(c, r0, key, w):
            pos = (lax.broadcasted_iota(jnp.int32, (MOE_CH, MOE_WIN), 1) + r0).astype(F32)
            sel = jnp.where(key == pos, 1.0, 0.0).astype(BF16)
            part = jnp.dot(sel, y_ref[row_block(r0, MOE_WIN), :], preferred_element_type=F32)
            o_ref[c * MOE_CH:(c + 1) * MOE_CH, :] += w * part

        starts = [windows(c) for c in range(nch)]
        cols = []
        for c in range(nch):
            tok = slice(c * MOE_CH, (c + 1) * MOE_CH)
            key = jnp.sum(jnp.where(lane_e, key_ref[tok, :], 0.0), axis=-1, keepdims=True)
            w = jnp.sum(jnp.where(lane_e, comb_ref[tok, :], 0.0), axis=-1, keepdims=True)
            cols.append((key, w))
            scatter(c, starts[c][0], key, w)
        for c in range(nch):
            w0, n_win = starts[c]
            key, w = cols[c]
            lax.fori_loop(1, n_win,
                          lambda k, carry, c=c, w0=w0, key=key, w=w:
                          (scatter(c, w0 + k * MOE_WIN, key, w), carry)[1], 0)

    @pl.when((e == pl.num_programs(1) - 1) & (f == last_f))
    def _():
        o_ref[...] = _gate_residual(x_ref[...], o_ref[...], mod_ref[...])


def _ffn_moe(h, x, router_pad, w_gu_bf, w_down_bf, mods, m, l, seg, fc=1408):
    rows = x.shape[0]
    d = D_MODEL
    nf = D_FF // fc
    tt = MOE_TILE
    nt = rows // tt
    nch = tt // MOE_CH
    comb, key, keyt, cs = _moe_route(h, router_pad, m, tt)
    counts = cs.reshape(nt, MOE_CS_ROWS, LANES)[:, :nch + 1, :N_EXPERTS].reshape(-1)
    once = pl.Buffered(1)
    grid_spec = pltpu.PrefetchScalarGridSpec(
        num_scalar_prefetch=1,
        grid=(nt, N_EXPERTS, nf),
        in_specs=[
            pl.BlockSpec((tt, d), lambda i, e, f, cs: (i, 0), pipeline_mode=once),
            pl.BlockSpec((tt, d), lambda i, e, f, cs: (i, 0), pipeline_mode=once),
            pl.BlockSpec((tt, LANES), lambda i, e, f, cs: (i, 0), pipeline_mode=once),
            pl.BlockSpec((tt, LANES), lambda i, e, f, cs: (i, 0), pipeline_mode=once),
            pl.BlockSpec((N_EXPERTS, nch, 1, MOE_CH), lambda i, e, f, cs: (0, i, 0, 0), pipeline_mode=once),
            pl.BlockSpec((None, None, d, fc), lambda i, e, f, cs: (m, e, 0, f)),
            pl.BlockSpec((None, None, d, fc), lambda i, e, f, cs: (m, e, 0, nf + f)),
            pl.BlockSpec((None, None, fc, d), lambda i, e, f, cs: (m, e, f, 0)),
            pl.BlockSpec((None, SUBLANES, d), lambda i, e, f, cs: (l, seg, 5)),
        ],
        out_specs=pl.BlockSpec((tt, d), lambda i, e, f, cs: (i, 0), pipeline_mode=once),
        scratch_shapes=[pltpu.VMEM((tt + MOE_SB + MOE_WIN, d), BF16),
                        pltpu.VMEM((tt + MOE_SB + MOE_WIN, d), BF16)],
    )
    return pl.pallas_call(
        functools.partial(_moe_kernel, tt=tt),
        out_shape=jax.ShapeDtypeStruct((rows, d), F32),
        grid_spec=grid_spec,
        compiler_params=_cparams(("arbitrary", "arbitrary", "arbitrary")),
        name="ffn_moe",
    )(counts, h, x, comb, key, keyt, w_gu_bf, w_gu_bf, w_down_bf, mods)


def _final_kernel(x_ref, g_ref, o_ref, *, batch):
    for b in range(batch):
        o_ref[b] = _rms(x_ref[:, b, :], g_ref[...])


def _final_norm(x, norm_final, batch, tl=64):
    rows, d = x.shape
    length = rows // batch
    return pl.pallas_call(
        functools.partial(_final_kernel, batch=batch),
        out_shape=jax.ShapeDtypeStruct((batch, length, d), F32),
        grid=(length // tl,),
        in_specs=[pl.BlockSpec((tl, batch, d), lambda i: (i, 0, 0)),
                  pl.BlockSpec((1, d), lambda i: (0, 0))],
        out_specs=pl.BlockSpec((batch, tl, d), lambda i: (0, i, 0)),
        compiler_params=_cparams(("arbitrary",)),
        name="final_norm",
    )(x.reshape(length, batch, d), norm_final.reshape(1, d))


def _time_major_kernel(x_ref, o_ref, *, batch):
    for b in range(batch):
        o_ref[:, b, :] = x_ref[b]


def _to_time_major(x, tl=64):
    batch, length, d = x.shape
    out = pl.pallas_call(
        functools.partial(_time_major_kernel, batch=batch),
        out_shape=jax.ShapeDtypeStruct((length, batch, d), x.dtype),
        grid=(length // tl,),
        in_specs=[pl.BlockSpec((batch, tl, d), lambda i: (0, i, 0))],
        out_specs=pl.BlockSpec((tl, batch, d), lambda i: (i, 0, 0)),
        compiler_params=_cparams(("arbitrary",)),
        name="to_time_major",
    )(x)
    return out.reshape(length * batch, d)


def _block_diag_heads(w):
    hh = N_LRU_HEADS // 2
    w = w.reshape(DEPTH, 2, 2, hh, LRU_HEAD_DIM, LRU_HEAD_DIM)
    eye = jnp.eye(hh, dtype=w.dtype)
    bd = jnp.einsum('ldgpij,pq->ldgpiqj', w, eye)
    return bd.reshape(DEPTH, 2, 2, hh * LRU_HEAD_DIM, hh * LRU_HEAD_DIM)


def kernel(x_prompt, x_sample, state_lru, c, c_ctx, norm_mix, norm_ffn, norm_final, ada_w, ada_b,
           w_in, w_out, hyena_short_w, hyena_short_b, filt_w1, filt_b1, filt_w2, filt_b2, filt_w3,
           filt_b3, filt_freq, hyena_bias, lru_conv_w, lru_conv_b, lru_wa, lru_ba, lru_wx, lru_bx,
           lru_lambda, ffn_w_gu, ffn_w_down, moe_router, moe_w_gu, moe_w_down):
    bp, lp, d = x_prompt.shape
    bs, ls, _ = x_sample.shape
    assert bs == SUBLANES and bp % SUBLANES == 0 and ls == (FFT_N1 // 2) * FFT_N2

    xs = _to_time_major(x_sample)
    xp = _to_time_major(x_prompt)

    cv = jnp.concatenate([c, jnp.broadcast_to(c_ctx[None, :], (SUBLANES, d))], axis=0)
    mods = _ada_mods(cv, ada_w, ada_b)

    w_in_bf = w_in.astype(BF16)
    w_out_bf = w_out.astype(BF16)
    ffn_gu_bf = ffn_w_gu.astype(BF16)
    ffn_down_bf = ffn_w_down.astype(BF16)
    moe_gu_bf = moe_w_gu.astype(BF16)
    moe_down_bf = moe_w_down.astype(BF16)
    router_pad = jnp.pad(moe_router, ((0, 0), (0, 0), (0, LANES - N_EXPERTS)))
    wbd = jnp.stack([_block_diag_heads(lru_wa), _block_diag_heads(lru_wx)], axis=2).astype(BF16)
    filt = dict(filt_w1=filt_w1, filt_b1=filt_b1, filt_w2=filt_w2, filt_b2=filt_b2,
                filt_w3=filt_w3, filt_b3=filt_b3, filt_freq=filt_freq)
    state_flat = state_lru.reshape(bs, DEPTH * 2 * D_LRU)
    zero_state = jnp.zeros((bp, D_LRU), F32)

    new_states = []
    for l in range(DEPTH):
        m = l // 2
        for path in range(2):
            if path == 0:
                x, batch, seg, grid_pos = xp, bp, 1, False
            else:
                x, batch, seg, grid_pos = xs, bs, 0, True
            length = x.shape[0] // batch

            u, x2, xr, pg = _in_proj_convs(x, norm_mix, mods, w_in_bf, hyena_short_w, hyena_short_b,
                                           lru_conv_w, lru_conv_b, l, seg, batch)

            ks, kd, ksum = _hyena_filter_taps(length, grid_pos, filt, l, n2_major=(path == 1))
            if path == 0:
                khat = _filter_spectrum_direct(ks, kd, ksum)
                hy = _hyena_conv_direct(u, x2, khat, hyena_bias, l, batch)
            else:
                khat = _filter_spectrum_2stage(ks, kd, ksum)
                hy = _hyena_conv_2stage(u, x2, khat, hyena_bias, l, batch)

            lru_tm = min(2048, x.shape[0])
            if path == 0:
                h0f, h0f_blk, h0r, h0r_blk = zero_state, (0, 0), zero_state, (0, 0)
            else:
                h0f, h0f_blk, h0r, h0r_blk = state_flat, (0, 2 * l), state_flat, (0, 2 * l + 1)
            hf, hlast_f = _lru_scan(xr, wbd, lru_ba, lru_bx, lru_lambda, h0f, h0f_blk, l, 0, batch,
                                    tm=lru_tm)
            lr, hlast_r = _lru_scan(xr, wbd, lru_ba, lru_bx, lru_lambda, h0r, h0r_blk, l, 1, batch,
                                    hf=hf, pg=pg, tm=lru_tm)
            if path == 0:
                new_states.append(jnp.stack([hlast_f, hlast_r], axis=1))

            x, h = _out_proj(hy, lr, x, w_out_bf, norm_ffn, mods, l, seg)
            if l % 2 == 0:
                x = _ffn_dense(h, x, ffn_gu_bf, ffn_down_bf, mods, m, l, seg)
            else:
                x = _ffn_moe(h, x, router_pad, moe_gu_bf, moe_down_bf, mods, m, l, seg)
            if path == 0:
                xp = x
            else:
                xs = x

    new_state_lru = jnp.stack(new_states, axis=1).astype(state_lru.dtype)
    y_prompt = _final_norm(xp, norm_final, bp)
    y_sample = _final_norm(xs, norm_final, bs)
    return (y_prompt, y_sample, new_state_lru)
```

```python
import functools
import math

import numpy as np
import jax
import jax.numpy as jnp
from jax import lax
from jax.experimental import pallas as pl
from jax.experimental.pallas import tpu as pltpu

F32 = jnp.float32
BF16 = jnp.bfloat16

D_MODEL = 1024
DEPTH = 4
D_HYENA = 512
D_LRU = 512
N_IN = 3 * D_HYENA + 2 * D_LRU
N_LRU_HEADS = 8
LRU_HEAD_DIM = D_LRU // N_LRU_HEADS
LRU_C = 8.0
FILTER_HIDDEN = 64
N_BANDS = 4
POS_DIM = 1 + 2 * N_BANDS + 3
GRID_W = 64
HYENA_DECAY_FAST = 0.3
HYENA_DECAY_SLOW = 1.5
HYENA_DECAY_TARGET = 1e-2
HYENA_SHIFT = 0.05
D_FF = 2816
N_EXPERTS = 8
EPS = 1e-6

SUBLANES = 8
LANES = 128
VMEM_LIMIT = 56 * 1024 * 1024

FFT_N1 = 128
FFT_N2 = 64
K1_BLOCK = 8
N2_BLOCK = 8


def _cparams(sem):
    return pltpu.CompilerParams(dimension_semantics=sem, vmem_limit_bytes=VMEM_LIMIT)


def _split(a):
    hi = a.astype(BF16)
    lo = (a - hi.astype(F32)).astype(BF16)
    return hi, lo


def _dot(a, b, x3=False):
    if not x3:
        return jnp.dot(a.astype(BF16), b.astype(BF16), preferred_element_type=F32)
    ah, al = _split(a)
    bh, bl = _split(b)
    return (jnp.dot(ah, bh, preferred_element_type=F32)
            + jnp.dot(ah, bl, preferred_element_type=F32)
            + jnp.dot(al, bh, preferred_element_type=F32))


def _silu(x):
    return x * jax.nn.sigmoid(x)


def _gelu_tanh(x):
    return 0.5 * x * (1.0 + jnp.tanh(math.sqrt(2.0 / math.pi) * (x + 0.044715 * (x * x * x))))


def _rms(x, g):
    return x * lax.rsqrt(jnp.mean(x * x, axis=-1, keepdims=True) + EPS) * g


def _modulate(y, shift, scale):
    rows, d = y.shape
    y3 = y.reshape(rows // SUBLANES, SUBLANES, d)
    return (y3 * (1.0 + scale)[None] + shift[None]).reshape(rows, d)


@functools.lru_cache(maxsize=None)
def _dft_consts():
    n = FFT_N1 * FFT_N2
    half1 = FFT_N1 // 2
    k1 = np.arange(FFT_N1)[:, None]
    n1 = np.arange(half1)[None, :]
    ang = 2.0 * np.pi * k1 * n1 / FFT_N1
    c, s = np.cos(ang), np.sin(ang)
    m1 = np.block([[c, s], [-s, c]])
    k2 = np.arange(FFT_N2)[None, :, None]
    n2 = np.arange(FFT_N2)[None, None, :]
    kk1 = np.arange(FFT_N1)[:, None, None]
    th = 2.0 * np.pi * (n2 * k2 / FFT_N2 + n2 * kk1 / n)
    fr, fi = np.cos(th), -np.sin(th)
    g2 = np.concatenate([np.concatenate([fr, -fi], axis=2), np.concatenate([fi, fr], axis=2)], axis=1)
    g2t = np.transpose(g2, (0, 2, 1))
    cm, sm = c.T / n, s.T / n
    m3 = np.block([[cm, -sm], [sm, cm]])
    f32 = lambda a: np.asarray(a, np.float32)
    return f32(m1), f32(g2), f32(g2t), f32(m3)


@functools.lru_cache(maxsize=None)
def _inverse_stage1_interleaved(batch):
    _, _, _, m3 = _dft_consts()
    half1 = FFT_N1 // 2
    pairs = batch // 2
    big = np.zeros((half1, batch, pairs, 2 * FFT_N1), np.float32)
    for q in range(pairs):
        for part in range(2):
            big[:, 2 * q + part, q, :] = m3[part * half1:(part + 1) * half1]
    return big.reshape(half1 * batch, pairs * 2 * FFT_N1)


@functools.lru_cache(maxsize=None)
def _dft_consts_direct(length):
    n = 2 * length
    f = np.arange(n)[:, None]
    t = np.arange(length)[None, :]
    ang = 2.0 * np.pi * f * t / n
    c, s = np.cos(ang), np.sin(ang)
    fwd = np.block([[c, s], [-s, c]])
    ct, st = c.T / n, s.T / n
    inv = np.block([[ct, -st], [st, ct]])
    f32 = lambda a: np.asarray(a, np.float32)
    return f32(fwd), f32(inv), f32(c), f32(-s)


def _hyena_deltas():
    min_decay = math.log(HYENA_DECAY_TARGET) / HYENA_DECAY_SLOW
    max_decay = math.log(HYENA_DECAY_TARGET) / HYENA_DECAY_FAST
    return jnp.abs(jnp.linspace(min_decay, max_decay, D_HYENA, dtype=F32))


def _pos_feats(length, grid):
    d = jnp.arange(length, dtype=jnp.int32)
    t = d.astype(F32) / length
    bands = jnp.arange(1, N_BANDS + 1, dtype=F32)
    ang = 2.0 * math.pi * t[:, None] * bands[None, :]
    feats = [t[:, None], jnp.sin(ang), jnp.cos(ang)]
    if grid:
        rows = max(length // GRID_W, 1)
        col_ang = 2.0 * math.pi * (d % GRID_W).astype(F32) / GRID_W
        row = (d // GRID_W).astype(F32) / rows
        feats += [jnp.sin(col_ang)[:, None], jnp.cos(col_ang)[:, None], row[:, None]]
    else:
        feats.append(jnp.zeros((length, 3), F32))
    z = jnp.concatenate(feats, axis=-1)
    return jnp.pad(z, ((0, 0), (0, LANES - POS_DIM)))


def _ada_kernel(cv_ref, w_ref, b_ref, o_ref):
    s = _silu(cv_ref[...])
    o_ref[...] = _dot(s, w_ref[...]) + b_ref[...]


def _ada_mods(cv, ada_w, ada_b):
    tn = D_MODEL
    return pl.pallas_call(
        _ada_kernel,
        out_shape=jax.ShapeDtypeStruct((DEPTH, 16, 6 * D_MODEL), F32),
        grid=(DEPTH, 6 * D_MODEL // tn),
        in_specs=[
            pl.BlockSpec((16, D_MODEL), lambda l, j: (0, 0)),
            pl.BlockSpec((None, D_MODEL, tn), lambda l, j: (l, 0, j)),
            pl.BlockSpec((None, 1, tn), lambda l, j: (l, 0, j)),
        ],
        out_specs=pl.BlockSpec((None, 16, tn), lambda l, j: (l, 0, j)),
        compiler_params=_cparams(("arbitrary", "arbitrary")),
        name="ada_mods",
    )(cv, ada_w, ada_b.reshape(DEPTH, 1, 6 * D_MODEL))


def _in_kernel(x_ref, prev_ref, next_ref, g_ref, mod_ref, w_ref, sw_ref, sb_ref, cw_ref, cb_ref,
               u_ref, x2_ref, xr_ref, pg_ref, *, batch, tm):
    i = pl.program_id(0)
    last = pl.num_programs(0) - 1
    lead = 2 * batch
    xe = jnp.concatenate([prev_ref[...], x_ref[...], next_ref[...]], axis=0)
    h = _modulate(_rms(xe, g_ref[...]), mod_ref[:, 0:D_MODEL], mod_ref[:, D_MODEL:2 * D_MODEL])
    h = jnp.concatenate([h[:lead] * (i > 0).astype(F32), h[lead:lead + tm],
                         h[lead + tm:] * (i < last).astype(F32)], axis=0)
    p = jnp.dot(h.astype(BF16), w_ref[...], preferred_element_type=F32)

    def tap(d, c0, c1):
        return p[lead + d * batch:lead + d * batch + tm, c0:c1]

    def hyena_branch(j):
        c0, c1 = j * D_HYENA, (j + 1) * D_HYENA
        acc = sb_ref[:, c0:c1] + tap(-1, c0, c1) * sw_ref[0:1, c0:c1]
        acc = acc + tap(0, c0, c1) * sw_ref[1:2, c0:c1]
        return acc + tap(1, c0, c1) * sw_ref[2:3, c0:c1]

    u_ref[...] = hyena_branch(0) * hyena_branch(1)
    x2_ref[...] = hyena_branch(2)
    c0, c1 = 3 * D_HYENA, 3 * D_HYENA + D_LRU
    acc = cb_ref[...] + tap(-2, c0, c1) * cw_ref[0:1, :]
    acc = acc + tap(-1, c0, c1) * cw_ref[1:2, :]
    acc = acc + tap(0, c0, c1) * cw_ref[2:3, :]
    xr_ref[...] = acc + tap(1, c0, c1) * cw_ref[3:4, :]
    pg_ref[...] = tap(0, c1, c1 + D_LRU)


def _in_proj_convs(x, norm_mix, mods, w_in_bf, short_w, short_b, conv_w, conv_b, l, seg, batch, tm=512):
    rows = x.shape[0]
    per_prev = tm // (2 * batch)
    per_next = tm // batch
    n_next = rows // batch
    out = jax.ShapeDtypeStruct((rows, D_HYENA), F32)
    return pl.pallas_call(
        functools.partial(_in_kernel, batch=batch, tm=tm),
        out_shape=(out, out, out, out),
        grid=(rows // tm,),
        in_specs=[
            pl.BlockSpec((tm, D_MODEL), lambda i: (i, 0)),
            pl.BlockSpec((2 * batch, D_MODEL), lambda i: (jnp.maximum(i * per_prev - 1, 0), 0)),
            pl.BlockSpec((batch, D_MODEL), lambda i: (jnp.minimum((i + 1) * per_next, n_next - 1), 0)),
            pl.BlockSpec((None, 1, D_MODEL), lambda i: (l, 0, 0)),
            pl.BlockSpec((None, SUBLANES, 2 * D_MODEL), lambda i: (l, seg, 0)),
            pl.BlockSpec((None, D_MODEL, N_IN), lambda i: (l, 0, 0)),
            pl.BlockSpec((None, 3, 3 * D_HYENA), lambda i: (l, 0, 0)),
            pl.BlockSpec((None, 1, 3 * D_HYENA), lambda i: (l, 0, 0)),
            pl.BlockSpec((None, 4, D_LRU), lambda i: (l, 0, 0)),
            pl.BlockSpec((None, 1, D_LRU), lambda i: (l, 0, 0)),
        ],
        out_specs=(pl.BlockSpec((tm, D_HYENA), lambda i: (i, 0)),) * 4,
        compiler_params=_cparams(("arbitrary",)),
        name="in_proj_convs",
    )(x, x, x, norm_mix.reshape(DEPTH, 1, D_MODEL), mods, w_in_bf,
      short_w, short_b.reshape(DEPTH, 1, 3 * D_HYENA), conv_w, conv_b.reshape(DEPTH, 1, D_LRU))


def _lru_gates(xr, w_ref, ba, bx, sp):
    xb = xr.astype(BF16)
    half = D_LRU // 2
    ra, rx = [], []
    for hf in range(2):
        xs = xb[:, hf * half:(hf + 1) * half]
        ra.append(jnp.dot(xs, w_ref[0, hf], preferred_element_type=F32))
        rx.append(jnp.dot(xs, w_ref[1, hf], preferred_element_type=F32))
    ta = jnp.tanh(jnp.concatenate(ra, axis=1) + ba)
    tx = jnp.tanh(jnp.concatenate(rx, axis=1) + bx)
    c1 = (0.5 * LRU_C) * sp
    neg_log_a = c1 + c1 * ta
    a = jnp.exp2(neg_log_a * (-1.0 / math.log(2.0)))
    mult = jnp.sqrt(jnp.maximum(jnp.tanh(neg_log_a) * (1.0 + a * a), 0.0))
    return a, (0.5 * xr) * (1.0 + tx) * mult


def _softplus(x):
    return jnp.maximum(x, 0.0) + jnp.log1p(jnp.exp(-jnp.abs(x)))


def _lru_kernel(*refs, batch, tm, chunk, reverse):
    if reverse:
        (xr_ref, w_ref, ba_ref, bx_ref, lam_ref, h0_ref, hf_ref, pg_ref,
         y_ref, hl_ref, a_scr, b_scr, h_scr) = refs
    else:
        (xr_ref, w_ref, ba_ref, bx_ref, lam_ref, h0_ref,
         y_ref, hl_ref, a_scr, b_scr, h_scr) = refs
    i = pl.program_id(0)

    @pl.when(i == 0)
    def _():
        h_scr[...] = h0_ref[...].astype(F32)

    sp = _softplus(-lam_ref[...])
    ba = 0.5 * ba_ref[...]
    bx = 0.5 * bx_ref[...]

    def fill(c, carry):
        r = pl.multiple_of(c * chunk, chunk)
        a, b = _lru_gates(xr_ref[pl.ds(r, chunk), :], w_ref, ba, bx, sp)
        a_scr[pl.ds(r, chunk), :] = a
        b_scr[pl.ds(r, chunk), :] = b
        return carry

    lax.fori_loop(0, tm // chunk, fill, 0)

    steps = tm // batch

    def step(s, h):
        t = (steps - 1 - s) if reverse else s
        r = pl.multiple_of(t * batch, batch)
        h = a_scr[pl.ds(r, batch), :] * h + b_scr[pl.ds(r, batch), :]
        if reverse:
            y_ref[pl.ds(r, batch), :] = ((hf_ref[pl.ds(r, batch), :] + h)
                                         * _gelu_tanh(pg_ref[pl.ds(r, batch), :]))
        else:
            y_ref[pl.ds(r, batch), :] = h
        return h

    h = lax.fori_loop(0, steps, step, h_scr[...], unroll=8)
    h_scr[...] = h

    @pl.when(i == pl.num_programs(0) - 1)
    def _():
        hl_ref[...] = h


def _lru_scan(xr, wbd, lru_ba, lru_bx, lru_lambda, h0, h0_block, l, direction, batch,
              hf=None, pg=None, tm=2048, chunk=256):
    rows = xr.shape[0]
    nt = rows // tm
    reverse = direction == 1
    row_map = (lambda i: (nt - 1 - i, 0)) if reverse else (lambda i: (i, 0))
    vec = lambda: pl.BlockSpec((None, None, 1, D_LRU), lambda i: (l, direction, 0, 0))
    in_specs = [
        pl.BlockSpec((tm, D_LRU), row_map),
        pl.BlockSpec((None, None, 2, 2, D_LRU // 2, D_LRU // 2), lambda i: (l, direction, 0, 0, 0, 0)),
        vec(), vec(), vec(),
        pl.BlockSpec((batch, D_LRU), lambda i: h0_block),
    ]
    args = [xr, wbd, lru_ba.reshape(DEPTH, 2, 1, D_LRU), lru_bx.reshape(DEPTH, 2, 1, D_LRU),
            lru_lambda.reshape(DEPTH, 2, 1, D_LRU), h0]
    if reverse:
        in_specs += [pl.BlockSpec((tm, D_LRU), row_map), pl.BlockSpec((tm, D_LRU), row_map)]
        args += [hf, pg]
    return pl.pallas_call(
        functools.partial(_lru_kernel, batch=batch, tm=tm, chunk=chunk, reverse=reverse),
        out_shape=(jax.ShapeDtypeStruct((rows, D_LRU), F32), jax.ShapeDtypeStruct((batch, D_LRU), F32)),
        grid=(nt,),
        in_specs=in_specs,
        out_specs=(pl.BlockSpec((tm, D_LRU), row_map), pl.BlockSpec((batch, D_LRU), lambda i: (0, 0))),
        scratch_shapes=[pltpu.VMEM((tm, D_LRU), F32), pltpu.VMEM((tm, D_LRU), F32),
                        pltpu.VMEM((batch, D_LRU), F32)],
        compiler_params=_cparams(("arbitrary",)),
        name="lru_rev" if reverse else "lru_fwd",
    )(*args)


def _filter_kernel(z_ref, w1_ref, b1_ref, w2_ref, b2_ref, w3_ref, b3_ref, fr_ref, dl_ref,
                   ks_ref, kd_ref, sum_ref):
    i = pl.program_id(0)
    hi = lax.Precision.HIGHEST
    z = z_ref[...]
    h1 = jnp.sin(fr_ref[0:1, :] * (jnp.dot(z, w1_ref[...], precision=hi, preferred_element_type=F32)
                                   + b1_ref[...]))
    h2 = jnp.sin(fr_ref[1:2, :] * (jnp.dot(h1, w2_ref[...], precision=hi, preferred_element_type=F32)
                                   + b2_ref[...]))
    k = jnp.dot(h2, w3_ref[...], precision=hi, preferred_element_type=F32) + b3_ref[...]
    window = jnp.exp(-z[:, 0:1] * dl_ref[...]) + HYENA_SHIFT
    kf = k[:, :D_HYENA] * window
    kb = k[:, D_HYENA:] * window
    row = lax.broadcasted_iota(jnp.int32, kb.shape, 0) + i * kb.shape[0]
    kb = jnp.where(row == 0, 0.0, kb)
    ks_ref[...] = kf + kb
    kd_ref[...] = kf - kb
    part = jnp.sum(jnp.abs(kf) + jnp.abs(kb), axis=0, keepdims=True)

    @pl.when(i == 0)
    def _():
        sum_ref[...] = part

    @pl.when(i > 0)
    def _():
        sum_ref[...] += part


def _hyena_filter_taps(length, grid, p, l, tl=256, n2_major=False):
    z = _pos_feats(length, grid)
    if n2_major:
        z = z.reshape(length // FFT_N2, FFT_N2, LANES).transpose(1, 0, 2).reshape(length, LANES)
    w1 = jnp.pad(p['filt_w1'], ((0, 0), (0, LANES - POS_DIM), (0, 0)))
    tap = jax.ShapeDtypeStruct((length, D_HYENA), F32)
    h = FILTER_HIDDEN
    return pl.pallas_call(
        _filter_kernel,
        out_shape=(tap, tap, jax.ShapeDtypeStruct((1, D_HYENA), F32)),
        grid=(length // tl,),
        in_specs=[
            pl.BlockSpec((tl, LANES), lambda i: (i, 0)),
            pl.BlockSpec((None, LANES, h), lambda i: (l, 0, 0)),
            pl.BlockSpec((None, 1, h), lambda i: (l, 0, 0)),
            pl.BlockSpec((None, h, h), lambda i: (l, 0, 0)),
            pl.BlockSpec((None, 1, h), lambda i: (l, 0, 0)),
            pl.BlockSpec((None, h, 2 * D_HYENA), lambda i: (l, 0, 0)),
            pl.BlockSpec((None, 1, 2 * D_HYENA), lambda i: (l, 0, 0)),
            pl.BlockSpec((None, 2, h), lambda i: (l, 0, 0)),
            pl.BlockSpec((1, D_HYENA), lambda i: (0, 0)),
        ],
        out_specs=(pl.BlockSpec((tl, D_HYENA), lambda i: (i, 0)),
                   pl.BlockSpec((tl, D_HYENA), lambda i: (i, 0)),
                   pl.BlockSpec((1, D_HYENA), lambda i: (0, 0))),
        compiler_params=_cparams(("arbitrary",)),
        name="hyena_filter",
    )(z, w1, p['filt_b1'].reshape(DEPTH, 1, h), p['filt_w2'], p['filt_b2'].reshape(DEPTH, 1, h),
      p['filt_w3'], p['filt_b3'].reshape(DEPTH, 1, 2 * D_HYENA), p['filt_freq'],
      _hyena_deltas().reshape(1, D_HYENA))


def _spec_direct_kernel(c_ref, s_ref, ks_ref, kd_ref, sum_ref, o_ref):
    scale = 1.0 / (sum_ref[...] + EPS)
    o_ref[0] = _dot(c_ref[...], ks_ref[...], x3=True) * scale
    o_ref[1] = _dot(s_ref[...], kd_ref[...], x3=True) * scale


def _filter_spectrum_direct(ks, kd, ksum):
    length = ks.shape[0]
    n = 2 * length
    _, _, c, ms = _dft_consts_direct(length)
    return pl.pallas_call(
        _spec_direct_kernel,
        out_shape=jax.ShapeDtypeStruct((2, n, D_HYENA), F32),
        compiler_params=_cparams(()),
        name="filter_spectrum_direct",
    )(jnp.asarray(c), jnp.asarray(ms), ks, kd, ksum)


def _conv_direct_kernel(u_ref, x2_ref, kh_ref, fwd_ref, inv_ref, bias_ref, o_ref, *, length, batch):
    n = 2 * length
    kr, ki = kh_ref[0], kh_ref[1]
    for q in range(batch // 2):
        z = jnp.concatenate([u_ref[:, 2 * q, :], u_ref[:, 2 * q + 1, :]], axis=0)
        x = _dot(fwd_ref[...], z, x3=True)
        xr, xi = x[:n], x[n:]
        y = jnp.concatenate([xr * kr - xi * ki, xr * ki + xi * kr], axis=0)
        conv = _dot(inv_ref[...], y, x3=True)
        o_ref[:, 2 * q, :] = conv[:length]
        o_ref[:, 2 * q + 1, :] = conv[length:]
    o_ref[...] = x2_ref[...] * (o_ref[...] + u_ref[...] * bias_ref[...])


def _hyena_conv_direct(u, x2, khat, hyena_bias, l, batch, cw=256):
    rows = u.shape[0]
    length = rows // batch
    fwd, inv, _, _ = _dft_consts_direct(length)
    blk = pl.BlockSpec((length, batch, cw), lambda j: (0, 0, j))
    out = pl.pallas_call(
        functools.partial(_conv_direct_kernel, length=length, batch=batch),
        out_shape=jax.ShapeDtypeStruct((length, batch, D_HYENA), F32),
        grid=(D_HYENA // cw,),
        in_specs=[blk, blk,
                  pl.BlockSpec((2, 2 * length, cw), lambda j: (0, 0, j)),
                  pl.BlockSpec(fwd.shape, lambda j: (0, 0)),
                  pl.BlockSpec(inv.shape, lambda j: (0, 0)),
                  pl.BlockSpec((None, 1, cw), lambda j: (l, 0, j))],
        out_specs=blk,
        compiler_params=_cparams(("arbitrary",)),
        name="hyena_conv_direct",
    )(u.reshape(length, batch, D_HYENA), x2.reshape(length, batch, D_HYENA), khat,
      jnp.asarray(fwd), jnp.asarray(inv), hyena_bias.reshape(DEPTH, 1, D_HYENA))
    return out.reshape(rows, D_HYENA)


def _stage1_kernel(x_ref, m_ref, o_ref, *, batch):
    m = m_ref[...].astype(BF16)
    for j in range(N2_BLOCK):
        for q in range(batch // 2):
            z = jnp.concatenate([x_ref[:, j, 2 * q, :], x_ref[:, j, 2 * q + 1, :]], axis=0)
            o_ref[j, q] = jnp.dot(m, z.astype(BF16), preferred_element_type=F32)


def _fft_stage1(u, batch, cw=256):
    c = u.shape[1]
    half1 = FFT_N1 // 2
    pairs = batch // 2
    m1, _, _, _ = _dft_consts()
    return pl.pallas_call(
        functools.partial(_stage1_kernel, batch=batch),
        out_shape=jax.ShapeDtypeStruct((FFT_N2, pairs, 2 * FFT_N1, c), F32),
        grid=(FFT_N2 // N2_BLOCK, c // cw),
        in_specs=[pl.BlockSpec((half1, N2_BLOCK, batch, cw), lambda j, k: (0, j, 0, k)),
                  pl.BlockSpec(m1.shape, lambda j, k: (0, 0))],
        out_specs=pl.BlockSpec((N2_BLOCK, pairs, 2 * FFT_N1, cw), lambda j, k: (j, 0, 0, k)),
        compiler_params=_cparams(("arbitrary", "arbitrary")),
        name="fft_stage1",
    )(u.reshape(half1, FFT_N2, batch, c), jnp.asarray(m1))


def _stage1_filter_kernel(ks_ref, kd_ref, m_ref, o_ref):
    m = m_ref[:, :FFT_N1 // 2]
    for j in range(N2_BLOCK):
        for g, ref in enumerate((ks_ref, kd_ref)):
            o_ref[j, g] = _dot(m, ref[j], x3=True)


def _stage2_filter_kernel(a_ref, g_ref, sum_ref, o_ref):
    scale = 1.0 / (sum_ref[...] + EPS)
    for j in range(K1_BLOCK):
        bs = jnp.concatenate([a_ref[:, 0, 0, j, :], a_ref[:, 0, 1, j, :]], axis=0)
        bd = jnp.concatenate([a_ref[:, 1, 0, j, :], a_ref[:, 1, 1, j, :]], axis=0)
        o_ref[0, j] = _dot(g_ref[j], bs, x3=True)[:FFT_N2] * scale
        o_ref[1, j] = _dot(g_ref[j], bd, x3=True)[FFT_N2:] * scale


def _filter_spectrum_2stage(ks, kd, ksum):
    c = D_HYENA
    half1 = FFT_N1 // 2
    m1, g2, _, _ = _dft_consts()
    tap = pl.BlockSpec((N2_BLOCK, half1, c), lambda j: (j, 0, 0))
    a = pl.pallas_call(
        _stage1_filter_kernel,
        out_shape=jax.ShapeDtypeStruct((FFT_N2, 2, 2 * FFT_N1, c), F32),
        grid=(FFT_N2 // N2_BLOCK,),
        in_specs=[tap, tap, pl.BlockSpec(m1.shape, lambda j: (0, 0))],
        out_specs=pl.BlockSpec((N2_BLOCK, 2, 2 * FFT_N1, c), lambda j: (j, 0, 0, 0)),
        compiler_params=_cparams(("arbitrary",)),
        name="filter_spectrum_stage1",
    )(ks.reshape(FFT_N2, half1, c), kd.reshape(FFT_N2, half1, c), jnp.asarray(m1))
    return pl.pallas_call(
        _stage2_filter_kernel,
        out_shape=jax.ShapeDtypeStruct((2, FFT_N1, FFT_N2, c), F32),
        grid=(FFT_N1 // K1_BLOCK,),
        in_specs=[pl.BlockSpec((FFT_N2, 2, 2, K1_BLOCK, c), lambda i: (0, 0, 0, i, 0)),
                  pl.BlockSpec((K1_BLOCK, 2 * FFT_N2, 2 * FFT_N2), lambda i: (i, 0, 0)),
                  pl.BlockSpec((1, c), lambda i: (0, 0))],
        out_specs=pl.BlockSpec((2, K1_BLOCK, FFT_N2, c), lambda i: (0, i, 0, 0)),
        compiler_params=_cparams(("arbitrary",)),
        name="filter_spectrum_stage2",
    )(a.reshape(FFT_N2, 2, 2, FFT_N1, c), jnp.asarray(g2), ksum)


def _stage2_kernel(a_ref, kh_ref, g_ref, gt_ref, o_ref):
    xs = []
    for j in range(K1_BLOCK):
        b = jnp.concatenate([a_ref[:, 0, j, :], a_ref[:, 1, j, :]], axis=0)
        xs.append(_dot(g_ref[j], b))
    ys = []
    for j in range(K1_BLOCK):
        xr, xi = xs[j][:FFT_N2], xs[j][FFT_N2:]
        kr, ki = kh_ref[0, j], kh_ref[1, j]
        ys.append(jnp.concatenate([xr * kr - xi * ki, xr * ki + xi * kr], axis=0).astype(BF16))
    for j in range(K1_BLOCK):
        o_ref[j] = jnp.dot(gt_ref[j].astype(BF16), ys[j], preferred_element_type=F32)


def _fft_stage2(a, khat):
    _, g2, g2t, _ = _dft_consts()
    c = D_HYENA
    pairs = a.shape[1]
    gblk = pl.BlockSpec((K1_BLOCK, 2 * FFT_N2, 2 * FFT_N2), lambda i, q: (i, 0, 0))
    return pl.pallas_call(
        _stage2_kernel,
        out_shape=jax.ShapeDtypeStruct((FFT_N1, pairs, 2 * FFT_N2, c), F32),
        grid=(FFT_N1 // K1_BLOCK, pairs),
        in_specs=[pl.BlockSpec((FFT_N2, None, 2, K1_BLOCK, c), lambda i, q: (0, q, 0, i, 0)),
                  pl.BlockSpec((2, K1_BLOCK, FFT_N2, c), lambda i, q: (0, i, 0, 0)), gblk, gblk],
        out_specs=pl.BlockSpec((K1_BLOCK, None, 2 * FFT_N2, c), lambda i, q: (i, q, 0, 0)),
        compiler_params=_cparams(("arbitrary", "arbitrary")),
        name="fft_stage2",
    )(a.reshape(FFT_N2, pairs, 2, FFT_N1, c), khat, jnp.asarray(g2), jnp.asarray(g2t))


def _stage3_kernel(a_ref, m_ref, u_ref, x2_ref, bias_ref, o_ref, *, batch):
    half1 = FFT_N1 // 2
    m = m_ref[...].astype(BF16)
    for j in range(N2_BLOCK):
        b = jnp.concatenate([a_ref[:, q, ri, j, :] for q in range(batch // 2) for ri in range(2)], axis=0)
        y = jnp.dot(m, b.astype(BF16), preferred_element_type=F32)
        o_ref[:, j] = y.reshape(half1, batch, y.shape[-1])
    o_ref[...] = x2_ref[...] * (o_ref[...] + u_ref[...] * bias_ref[...])


def _fft_stage3(a, u, x2, hyena_bias, l, batch, cw=256):
    rows, c = u.shape
    half1 = FFT_N1 // 2
    pairs = batch // 2
    m3 = _inverse_stage1_interleaved(batch)
    blk = pl.BlockSpec((half1, N2_BLOCK, batch, cw), lambda j, k: (0, j, 0, k))
    out = pl.pallas_call(
        functools.partial(_stage3_kernel, batch=batch),
        out_shape=jax.ShapeDtypeStruct((half1, FFT_N2, batch, c), F32),
        grid=(FFT_N2 // N2_BLOCK, c // cw),
        in_specs=[pl.BlockSpec((FFT_N1, pairs, 2, N2_BLOCK, cw), lambda j, k: (0, 0, 0, j, k)),
                  pl.BlockSpec(m3.shape, lambda j, k: (0, 0)),
                  blk, blk,
                  pl.BlockSpec((None, 1, cw), lambda j, k: (l, 0, k))],
        out_specs=blk,
        compiler_params=_cparams(("arbitrary", "arbitrary")),
        name="fft_stage3",
    )(a.reshape(FFT_N1, pairs, 2, FFT_N2, c), jnp.asarray(m3), u.reshape(half1, FFT_N2, batch, c),
      x2.reshape(half1, FFT_N2, batch, c), hyena_bias.reshape(DEPTH, 1, c))
    return out.reshape(rows, c)


def _hyena_conv_2stage(u, x2, khat, hyena_bias, l, batch):
    a = _fft_stage1(u, batch)
    a = _fft_stage2(a, khat)
    return _fft_stage3(a, u, x2, hyena_bias, l, batch)


def _mix_and_prenorm(hy_ref, lr_ref, x_ref, w_ref, g_ref, mod_ref):
    d = D_MODEL
    mixed = (jnp.dot(hy_ref[...].astype(BF16), w_ref[0:D_HYENA, :], preferred_element_type=F32)
             + jnp.dot(lr_ref[...].astype(BF16), w_ref[D_HYENA:, :], preferred_element_type=F32))
    rows = mixed.shape[0]
    g1 = mod_ref[:, 2 * d:3 * d]
    x = x_ref[...] + (mixed.reshape(rows // SUBLANES, SUBLANES, d) * g1[None]).reshape(rows, d)
    h = _modulate(_rms(x, g_ref[...]), mod_ref[:, 3 * d:4 * d], mod_ref[:, 4 * d:5 * d])
    return x, h.astype(BF16)


def _out_kernel(hy_ref, lr_ref, x_ref, w_ref, g_ref, mod_ref, xo_ref, h_ref):
    xo_ref[...], h_ref[...] = _mix_and_prenorm(hy_ref, lr_ref, x_ref, w_ref, g_ref, mod_ref)


def _out_proj(hy, lr, x, w_out_bf, norm_ffn, mods, l, seg, tm=512):
    rows = x.shape[0]
    d = D_MODEL
    return pl.pallas_call(
        _out_kernel,
        out_shape=(jax.ShapeDtypeStruct((rows, d), F32), jax.ShapeDtypeStruct((rows, d), BF16)),
        grid=(rows // tm,),
        in_specs=[
            pl.BlockSpec((tm, D_HYENA), lambda i: (i, 0)),
            pl.BlockSpec((tm, D_LRU), lambda i: (i, 0)),
            pl.BlockSpec((tm, d), lambda i: (i, 0)),
            pl.BlockSpec((None, D_HYENA + D_LRU, d), lambda i: (l, 0, 0)),
            pl.BlockSpec((None, 1, d), lambda i: (l, 0, 0)),
            pl.BlockSpec((None, SUBLANES, 6 * d), lambda i: (l, seg, 0)),
        ],
        out_specs=(pl.BlockSpec((tm, d), lambda i: (i, 0)), pl.BlockSpec((tm, d), lambda i: (i, 0))),
        compiler_params=_cparams(("arbitrary",)),
        name="out_proj",
    )(hy, lr, x, w_out_bf, norm_ffn.reshape(DEPTH, 1, d), mods)


def _gate_residual(x, acc, g2):
    rows, d = acc.shape
    return x + (acc.reshape(rows // SUBLANES, SUBLANES, d) * g2[None]).reshape(rows, d)


def _mix_ffn_kernel(hy_ref, lr_ref, x_ref, w_ref, g_ref, mod_ref, wg_ref, wu_ref, wd_ref,
                    o_ref, x_scr, h_scr, acc_ref):
    f = pl.program_id(1)

    @pl.when(f == 0)
    def _():
        x_scr[...], h_scr[...] = _mix_and_prenorm(hy_ref, lr_ref, x_ref, w_ref, g_ref, mod_ref)

    h = h_scr[...]
    g = jnp.dot(h, wg_ref[...], preferred_element_type=F32)
    u = jnp.dot(h, wu_ref[...], preferred_element_type=F32)
    part = jnp.dot((_silu(g) * u).astype(BF16), wd_ref[...], preferred_element_type=F32)

    @pl.when(f == 0)
    def _():
        acc_ref[...] = part

    @pl.when(f > 0)
    def _():
        acc_ref[...] += part

    @pl.when(f == pl.num_programs(1) - 1)
    def _():
        o_ref[...] = _gate_residual(x_scr[...], acc_ref[...], mod_ref[:, 5 * D_MODEL:])


def _out_proj_ffn_dense(hy, lr, x, w_out_bf, norm_ffn, w_gu_bf, w_down_bf, mods, m, l, seg, tm=512, fc=1408):
    rows = x.shape[0]
    d = D_MODEL
    nf = D_FF // fc
    return pl.pallas_call(
        _mix_ffn_kernel,
        out_shape=jax.ShapeDtypeStruct((rows, d), F32),
        grid=(rows // tm, nf),
        in_specs=[
            pl.BlockSpec((tm, D_HYENA), lambda i, f: (i, 0)),
            pl.BlockSpec((tm, D_LRU), lambda i, f: (i, 0)),
            pl.BlockSpec((tm, d), lambda i, f: (i, 0)),
            pl.BlockSpec((None, D_HYENA + D_LRU, d), lambda i, f: (l, 0, 0)),
            pl.BlockSpec((None, 1, d), lambda i, f: (l, 0, 0)),
            pl.BlockSpec((None, SUBLANES, 6 * d), lambda i, f: (l, seg, 0)),
            pl.BlockSpec((None, d, fc), lambda i, f: (m, 0, f)),
            pl.BlockSpec((None, d, fc), lambda i, f: (m, 0, nf + f)),
            pl.BlockSpec((None, fc, d), lambda i, f: (m, f, 0)),
        ],
        out_specs=pl.BlockSpec((tm, d), lambda i, f: (i, 0)),
        scratch_shapes=[pltpu.VMEM((tm, d), F32), pltpu.VMEM((tm, d), BF16), pltpu.VMEM((tm, d), F32)],
        compiler_params=_cparams(("arbitrary", "arbitrary")),
        name="out_proj_ffn_dense",
    )(hy, lr, x, w_out_bf, norm_ffn.reshape(DEPTH, 1, d), mods, w_gu_bf, w_gu_bf, w_down_bf)


def _route_top2(logits):
    lane = lax.broadcasted_iota(jnp.int32, logits.shape, 1)
    neg = jnp.float32(-jnp.inf)
    v = jnp.where(lane < N_EXPERTS, logits, neg)
    m1 = jnp.max(v, axis=-1, keepdims=True)
    i1 = jnp.min(jnp.where(v == m1, lane, LANES), axis=-1, keepdims=True)
    v2 = jnp.where(lane == i1, neg, v)
    m2 = jnp.max(v2, axis=-1, keepdims=True)
    i2 = jnp.min(jnp.where(v2 == m2, lane, LANES), axis=-1, keepdims=True)
    e2 = jnp.exp(m2 - m1)
    w1 = 1.0 / (1.0 + e2)
    w2 = e2 / (1.0 + e2)
    first = lane == i1
    second = lane == i2
    comb = jnp.where(first, w1, 0.0) + jnp.where(second, w2, 0.0)
    return comb, jnp.where(first | second, 1.0, 0.0)


MOE_TILE = 2048
MOE_CH = 256
MOE_SB = 128
MOE_WIN = 256
MOE_CS_ROWS = 16


def _router_kernel(h_ref, r_ref, comb_ref, key_ref, keyt_ref, cs_ref, *, tt):
    nch = tt // MOE_CH
    r_hi, r_lo = _split(r_ref[...])
    h = h_ref[...]
    logits = (jnp.dot(h, r_hi, preferred_element_type=F32) + jnp.dot(h, r_lo, preferred_element_type=F32))
    comb, sel = _route_top2(logits)
    comb_ref[...] = comb
    rr = lax.broadcasted_iota(jnp.int32, (MOE_CH, MOE_CH), 0)
    cc = lax.broadcasted_iota(jnp.int32, (MOE_CH, MOE_CH), 1)
    strict_lower = jnp.where(cc < rr, 1.0, 0.0).astype(BF16)
    cs_ref[...] = jnp.zeros(cs_ref.shape, jnp.int32)
    carry = jnp.zeros((1, LANES), F32)
    for c in range(nch):
        selc = sel[c * MOE_CH:(c + 1) * MOE_CH]
        cs_ref[c:c + 1, :] = carry.astype(jnp.int32)
        rank = jnp.dot(strict_lower, selc.astype(BF16), preferred_element_type=F32) + carry
        key_ref[c * MOE_CH:(c + 1) * MOE_CH, :] = jnp.where(selc > 0.0, rank, -1.0)
        carry = carry + jnp.sum(selc, axis=0, keepdims=True)
    cs_ref[nch:nch + 1, :] = carry.astype(jnp.int32)
    keyt = jnp.transpose(key_ref[...])
    for e in range(N_EXPERTS):
        for c in range(nch):
            keyt_ref[e, c] = keyt[e:e + 1, c * MOE_CH:(c + 1) * MOE_CH]


def _moe_route(h, router_pad, m, tt):
    rows, d = h.shape
    nt = rows // tt
    nch = tt // MOE_CH
    tok = jax.ShapeDtypeStruct((rows, LANES), F32)
    return pl.pallas_call(
        functools.partial(_router_kernel, tt=tt),
        out_shape=(tok, tok,
                   jax.ShapeDtypeStruct((N_EXPERTS, rows // MOE_CH, 1, MOE_CH), F32),
                   jax.ShapeDtypeStruct((nt * MOE_CS_ROWS, LANES), jnp.int32)),
        grid=(nt,),
        in_specs=[pl.BlockSpec((tt, d), lambda i: (i, 0)),
                  pl.BlockSpec((None, d, LANES), lambda i: (m, 0, 0))],
        out_specs=(pl.BlockSpec((tt, LANES), lambda i: (i, 0)),
                   pl.BlockSpec((tt, LANES), lambda i: (i, 0)),
                   pl.BlockSpec((N_EXPERTS, nch, 1, MOE_CH), lambda i: (0, i, 0, 0)),
                   pl.BlockSpec((MOE_CS_ROWS, LANES), lambda i: (i, 0))),
        compiler_params=_cparams(("arbitrary",)),
        name="moe_route",
    )(h, router_pad)


def _moe_kernel(cs_ref, h_ref, x_ref, comb_ref, key_ref, keyt_ref, wg_ref, wu_ref, wd_ref, mod_ref,
                o_ref, xg_ref, y_ref, *, tt):
    i = pl.program_id(0)
    e = pl.program_id(1)
    f = pl.program_id(2)
    last_f = pl.num_programs(2) - 1
    nch = tt // MOE_CH
    base = i * ((nch + 1) * N_EXPERTS) + e

    def count_before(c):
        return cs_ref[base + c * N_EXPERTS]

    n_sub = (count_before(nch) + (MOE_SB - 1)) // MOE_SB

    def windows(c):
        lo = count_before(c)
        hi = count_before(c + 1)
        w0 = (lo // MOE_SB) * MOE_SB
        return w0, (hi - w0 + (MOE_WIN - 1)) // MOE_WIN

    def row_block(r0, size):
        return pl.ds(pl.multiple_of(r0, MOE_SB), size)

    @pl.when((e == 0) & (f == 0))
    def _():
        o_ref[...] = jnp.zeros_like(o_ref)

    @pl.when(f == 0)
    def _():
        def clear(j, carry):
            xg_ref[row_block(j * MOE_SB, MOE_SB), :] = jnp.zeros((MOE_SB, D_MODEL), BF16)
            return carry

        lax.fori_loop(0, n_sub + MOE_WIN // MOE_SB, clear, 0)
        y_ref[row_block(n_sub * MOE_SB, MOE_WIN), :] = jnp.zeros((MOE_WIN, D_MODEL), BF16)

        def gather(c, r0):
            row = (lax.broadcasted_iota(jnp.int32, (MOE_WIN, MOE_CH), 0) + r0).astype(F32)
            sel = jnp.where(keyt_ref[e, c] == row, 1.0, 0.0).astype(BF16)
            part = jnp.dot(sel, h_ref[c * MOE_CH:(c + 1) * MOE_CH, :], preferred_element_type=F32)
            xg_ref[row_block(r0, MOE_WIN), :] += part.astype(BF16)

        starts = [windows(c) for c in range(nch)]
        for c in range(nch):
            gather(c, starts[c][0])
        for c in range(nch):
            w0, n_win = starts[c]
            lax.fori_loop(1, n_win, lambda k, carry, c=c, w0=w0: (gather(c, w0 + k * MOE_WIN), carry)[1], 0)

    def expert_rows(r0, size):
        rows = row_block(r0, size)
        xg = xg_ref[rows, :]
        g = jnp.dot(xg, wg_ref[...], preferred_element_type=F32)
        u = jnp.dot(xg, wu_ref[...], preferred_element_type=F32)
        y = jnp.dot((_silu(g) * u).astype(BF16), wd_ref[...], preferred_element_type=F32)

        @pl.when(f == 0)
        def _():
            y_ref[rows, :] = y.astype(BF16)

        @pl.when(f > 0)
        def _():
            y_ref[rows, :] = (y_ref[rows, :].astype(F32) + y).astype(BF16)

    n_pair = n_sub // 2
    lax.fori_loop(0, n_pair, lambda j, carry: (expert_rows(j * (2 * MOE_SB), 2 * MOE_SB), carry)[1], 0)

    @pl.when(n_sub % 2 == 1)
    def _():
        expert_rows(n_pair * (2 * MOE_SB), MOE_SB)

    @pl.when(f == last_f)
    def _():
        lane_e = lax.broadcasted_iota(jnp.int32, (MOE_CH, LANES), 1) == e

        def scatter(c, r0, key, w):
            pos = (lax.broadcasted_iota(jnp.int32, (MOE_CH, MOE_WIN), 1) + r0).astype(F32)
            sel = jnp.where(key == pos, 1.0, 0.0).astype(BF16)
            part = jnp.dot(sel, y_ref[row_block(r0, MOE_WIN), :], preferred_element_type=F32)
            o_ref[c * MOE_CH:(c + 1) * MOE_CH, :] += w * part

        starts = [windows(c) for c in range(nch)]
        cols = []
        for c in range(nch):
            tok = slice(c * MOE_CH, (c + 1) * MOE_CH)
            key = jnp.sum(jnp.where(lane_e, key_ref[tok, :], 0.0), axis=-1, keepdims=True)
            w = jnp.sum(jnp.where(lane_e, comb_ref[tok, :], 0.0), axis=-1, keepdims=True)
            cols.append((key, w))
            scatter(c, starts[c][0], key, w)
        for c in range(nch):
            w0, n_win = starts[c]
            key, w = cols[c]
            lax.fori_loop(1, n_win,
                          lambda k, carry, c=c, w0=w0, key=key, w=w:
                          (scatter(c, w0 + k * MOE_WIN, key, w), carry)[1], 0)

    @pl.when((e == pl.num_programs(1) - 1) & (f == last_f))
    def _():
        o_ref[...] = _gate_residual(x_ref[...], o_ref[...], mod_ref[...])


def _ffn_moe(h, x, router_pad, w_gu_bf, w_down_bf, mods, m, l, seg, fc=1408):
    rows = x.shape[0]
    d = D_MODEL
    nf = D_FF // fc
    tt = MOE_TILE
    nt = rows // tt
    nch = tt // MOE_CH
    comb, key, keyt, cs = _moe_route(h, router_pad, m, tt)
    counts = cs.reshape(nt, MOE_CS_ROWS, LANES)[:, :nch + 1, :N_EXPERTS].reshape(-1)
    once = pl.Buffered(1)
    grid_spec = pltpu.PrefetchScalarGridSpec(
        num_scalar_prefetch=1,
        grid=(nt, N_EXPERTS, nf),
        in_specs=[
            pl.BlockSpec((tt, d), lambda i, e, f, cs: (i, 0), pipeline_mode=once),
            pl.BlockSpec((tt, d), lambda i, e, f, cs: (i, 0), pipeline_mode=once),
            pl.BlockSpec((tt, LANES), lambda i, e, f, cs: (i, 0), pipeline_mode=once),
            pl.BlockSpec((tt, LANES), lambda i, e, f, cs: (i, 0), pipeline_mode=once),
            pl.BlockSpec((N_EXPERTS, nch, 1, MOE_CH), lambda i, e, f, cs: (0, i, 0, 0), pipeline_mode=once),
            pl.BlockSpec((None, None, d, fc), lambda i, e, f, cs: (m, e, 0, f)),
            pl.BlockSpec((None, None, d, fc), lambda i, e, f, cs: (m, e, 0, nf + f)),
            pl.BlockSpec((None, None, fc, d), lambda i, e, f, cs: (m, e, f, 0)),
            pl.BlockSpec((None, SUBLANES, d), lambda i, e, f, cs: (l, seg, 5)),
        ],
        out_specs=pl.BlockSpec((tt, d), lambda i, e, f, cs: (i, 0), pipeline_mode=once),
        scratch_shapes=[pltpu.VMEM((tt + MOE_SB + MOE_WIN, d), BF16),
                        pltpu.VMEM((tt + MOE_SB + MOE_WIN, d), BF16)],
    )
    return pl.pallas_call(
        functools.partial(_moe_kernel, tt=tt),
        out_shape=jax.ShapeDtypeStruct((rows, d), F32),
        grid_spec=grid_spec,
        compiler_params=_cparams(("arbitrary", "arbitrary", "arbitrary")),
        name="ffn_moe",
    )(counts, h, x, comb, key, keyt, w_gu_bf, w_gu_bf, w_down_bf, mods)


def _final_kernel(x_ref, g_ref, o_ref, *, batch):
    for b in range(batch):
        o_ref[b] = _rms(x_ref[:, b, :], g_ref[...])


def _final_norm(x, norm_final, batch, tl=64):
    rows, d = x.shape
    length = rows // batch
    return pl.pallas_call(
        functools.partial(_final_kernel, batch=batch),
        out_shape=jax.ShapeDtypeStruct((batch, length, d), F32),
        grid=(length // tl,),
        in_specs=[pl.BlockSpec((tl, batch, d), lambda i: (i, 0, 0)),
                  pl.BlockSpec((1, d), lambda i: (0, 0))],
        out_specs=pl.BlockSpec((batch, tl, d), lambda i: (0, i, 0)),
        compiler_params=_cparams(("arbitrary",)),
        name="final_norm",
    )(x.reshape(length, batch, d), norm_final.reshape(1, d))


def _time_major_kernel(x_ref, o_ref, *, batch):
    for b in range(batch):
        o_ref[:, b, :] = x_ref[b]


def _to_time_major(x, tl=64):
    batch, length, d = x.shape
    out = pl.pallas_call(
        functools.partial(_time_major_kernel, batch=batch),
        out_shape=jax.ShapeDtypeStruct((length, batch, d), x.dtype),
        grid=(length // tl,),
        in_specs=[pl.BlockSpec((batch, tl, d), lambda i: (0, i, 0))],
        out_specs=pl.BlockSpec((tl, batch, d), lambda i: (i, 0, 0)),
        compiler_params=_cparams(("arbitrary",)),
        name="to_time_major",
    )(x)
    return out.reshape(length * batch, d)


def _block_diag_heads(w):
    hh = N_LRU_HEADS // 2
    w = w.reshape(DEPTH, 2, 2, hh, LRU_HEAD_DIM, LRU_HEAD_DIM)
    eye = jnp.eye(hh, dtype=w.dtype)
    bd = jnp.einsum('ldgpij,pq->ldgpiqj', w, eye)
    return bd.reshape(DEPTH, 2, 2, hh * LRU_HEAD_DIM, hh * LRU_HEAD_DIM)


def kernel(x_prompt, x_sample, state_lru, c, c_ctx, norm_mix, norm_ffn, norm_final, ada_w, ada_b,
           w_in, w_out, hyena_short_w, hyena_short_b, filt_w1, filt_b1, filt_w2, filt_b2, filt_w3,
           filt_b3, filt_freq, hyena_bias, lru_conv_w, lru_conv_b, lru_wa, lru_ba, lru_wx, lru_bx,
           lru_lambda, ffn_w_gu, ffn_w_down, moe_router, moe_w_gu, moe_w_down):
    bp, lp, d = x_prompt.shape
    bs, ls, _ = x_sample.shape
    assert bs == SUBLANES and bp % SUBLANES == 0 and ls == (FFT_N1 // 2) * FFT_N2

    xs = _to_time_major(x_sample)
    xp = _to_time_major(x_prompt)

    cv = jnp.concatenate([c, jnp.broadcast_to(c_ctx[None, :], (SUBLANES, d))], axis=0)
    mods = _ada_mods(cv, ada_w, ada_b)

    w_in_bf = w_in.astype(BF16)
    w_out_bf = w_out.astype(BF16)
    ffn_gu_bf = ffn_w_gu.astype(BF16)
    ffn_down_bf = ffn_w_down.astype(BF16)
    moe_gu_bf = moe_w_gu.astype(BF16)
    moe_down_bf = moe_w_down.astype(BF16)
    router_pad = jnp.pad(moe_router, ((0, 0), (0, 0), (0, LANES - N_EXPERTS)))
    wbd = (0.5 * jnp.stack([_block_diag_heads(lru_wa), _block_diag_heads(lru_wx)], axis=2)).astype(BF16)
    filt = dict(filt_w1=filt_w1, filt_b1=filt_b1, filt_w2=filt_w2, filt_b2=filt_b2,
                filt_w3=filt_w3, filt_b3=filt_b3, filt_freq=filt_freq)
    state_flat = state_lru.reshape(bs, DEPTH * 2 * D_LRU)
    zero_state = jnp.zeros((bp, D_LRU), F32)

    new_states = []
    for l in range(DEPTH):
        m = l // 2
        for path in range(2):
            if path == 0:
                x, batch, seg, grid_pos = xp, bp, 1, False
            else:
                x, batch, seg, grid_pos = xs, bs, 0, True
            length = x.shape[0] // batch

            u, x2, xr, pg = _in_proj_convs(x, norm_mix, mods, w_in_bf, hyena_short_w, hyena_short_b,
                                           lru_conv_w, lru_conv_b, l, seg, batch)

            ks, kd, ksum = _hyena_filter_taps(length, grid_pos, filt, l, n2_major=(path == 1))
            if path == 0:
                khat = _filter_spectrum_direct(ks, kd, ksum)
                hy = _hyena_conv_direct(u, x2, khat, hyena_bias, l, batch)
            else:
                khat = _filter_spectrum_2stage(ks, kd, ksum)
                hy = _hyena_conv_2stage(u, x2, khat, hyena_bias, l, batch)

            lru_tm = min(2048, x.shape[0])
            if path == 0:
                h0f, h0f_blk, h0r, h0r_blk = zero_state, (0, 0), zero_state, (0, 0)
            else:
                h0f, h0f_blk, h0r, h0r_blk = state_flat, (0, 2 * l), state_flat, (0, 2 * l + 1)
            hf, hlast_f = _lru_scan(xr, wbd, lru_ba, lru_bx, lru_lambda, h0f, h0f_blk, l, 0, batch,
                                    tm=lru_tm)
            lr, hlast_r = _lru_scan(xr, wbd, lru_ba, lru_bx, lru_lambda, h0r, h0r_blk, l, 1, batch,
                                    hf=hf, pg=pg, tm=lru_tm)
            if path == 0:
                new_states.append(jnp.stack([hlast_f, hlast_r], axis=1))

            if l % 2 == 0:
                x = _out_proj_ffn_dense(hy, lr, x, w_out_bf, norm_ffn, ffn_gu_bf, ffn_down_bf, mods, m, l, seg)
            else:
                x, h = _out_proj(hy, lr, x, w_out_bf, norm_ffn, mods, l, seg)
                x = _ffn_moe(h, x, router_pad, moe_gu_bf, moe_down_bf, mods, m, l, seg)
            if path == 0:
                xp = x
            else:
                xs = x

    new_state_lru = jnp.stack(new_states, axis=1).astype(state_lru.dtype)
    y_prompt = _final_norm(xp, norm_final, bp)
    y_sample = _final_norm(xs, norm_final, bs)
    return (y_prompt, y_sample, new_state_lru)
```

```python
import functools
import math

import numpy as np
import jax
import jax.numpy as jnp
from jax import lax
from jax.experimental import pallas as pl
from jax.experimental.pallas import tpu as pltpu

F32 = jnp.float32
BF16 = jnp.bfloat16

D_MODEL = 1024
DEPTH = 4
D_HYENA = 512
D_LRU = 512
N_IN = 3 * D_HYENA + 2 * D_LRU
N_LRU_HEADS = 8
LRU_HEAD_DIM = D_LRU // N_LRU_HEADS
LRU_C = 8.0
FILTER_HIDDEN = 64
N_BANDS = 4
POS_DIM = 1 + 2 * N_BANDS + 3
GRID_W = 64
HYENA_DECAY_FAST = 0.3
HYENA_DECAY_SLOW = 1.5
HYENA_DECAY_TARGET = 1e-2
HYENA_SHIFT = 0.05
D_FF = 2816
N_EXPERTS = 8
EPS = 1e-6

SUBLANES = 8
LANES = 128
VMEM_LIMIT = 56 * 1024 * 1024

FFT_N1 = 128
FFT_N2 = 64
K1_BLOCK = 16
N2_BLOCK = 8
FFN_CHUNK = D_FF // 2


def _cparams(sem):
    return pltpu.CompilerParams(dimension_semantics=sem, vmem_limit_bytes=VMEM_LIMIT)


def _split(a):
    hi = a.astype(BF16)
    lo = (a - hi.astype(F32)).astype(BF16)
    return hi, lo


def _dot(a, b, x3=False):
    if not x3:
        return jnp.dot(a.astype(BF16), b.astype(BF16), preferred_element_type=F32)
    ah, al = _split(a)
    bh, bl = _split(b)
    return (jnp.dot(ah, bh, preferred_element_type=F32)
            + jnp.dot(ah, bl, preferred_element_type=F32)
            + jnp.dot(al, bh, preferred_element_type=F32))


def _silu(x):
    return x * jax.nn.sigmoid(x)


def _gelu_tanh(x):
    return 0.5 * x * (1.0 + jnp.tanh(math.sqrt(2.0 / math.pi) * (x + 0.044715 * (x * x * x))))


def _rms(x, g):
    return x * lax.rsqrt(jnp.mean(x * x, axis=-1, keepdims=True) + EPS) * g


def _modulate(y, shift, scale):
    rows, d = y.shape
    y3 = y.reshape(rows // SUBLANES, SUBLANES, d)
    return (y3 * (1.0 + scale)[None] + shift[None]).reshape(rows, d)


@functools.lru_cache(maxsize=None)
def _dft_consts():
    n = FFT_N1 * FFT_N2
    half1 = FFT_N1 // 2
    k1 = np.arange(FFT_N1)[:, None]
    n1 = np.arange(half1)[None, :]
    ang = 2.0 * np.pi * k1 * n1 / FFT_N1
    c, s = np.cos(ang), np.sin(ang)
    m1 = np.block([[c, s], [-s, c]])
    k2 = np.arange(FFT_N2)[None, :, None]
    n2 = np.arange(FFT_N2)[None, None, :]
    kk1 = np.arange(FFT_N1)[:, None, None]
    th = 2.0 * np.pi * (n2 * k2 / FFT_N2 + n2 * kk1 / n)
    fr, fi = np.cos(th), -np.sin(th)
    g2 = np.concatenate([np.concatenate([fr, -fi], axis=2), np.concatenate([fi, fr], axis=2)], axis=1)
    g2t = np.transpose(g2, (0, 2, 1))
    cm, sm = c.T / n, s.T / n
    m3 = np.block([[cm, -sm], [sm, cm]])
    f32 = lambda a: np.asarray(a, np.float32)
    return f32(m1), f32(g2), f32(g2t), f32(m3)


@functools.lru_cache(maxsize=None)
def _inverse_stage1_interleaved(batch):
    _, _, _, m3 = _dft_consts()
    half1 = FFT_N1 // 2
    pairs = batch // 2
    big = np.zeros((half1, batch, pairs, 2 * FFT_N1), np.float32)
    for q in range(pairs):
        for part in range(2):
            big[:, 2 * q + part, q, :] = m3[part * half1:(part + 1) * half1]
    return big.reshape(half1 * batch, pairs * 2 * FFT_N1)


@functools.lru_cache(maxsize=None)
def _dft_consts_direct(length):
    n = 2 * length
    f = np.arange(n)[:, None]
    t = np.arange(length)[None, :]
    ang = 2.0 * np.pi * f * t / n
    c, s = np.cos(ang), np.sin(ang)
    fwd = np.block([[c, s], [-s, c]])
    ct, st = c.T / n, s.T / n
    inv = np.block([[ct, -st], [st, ct]])
    f32 = lambda a: np.asarray(a, np.float32)
    return f32(fwd), f32(inv), f32(c), f32(-s)


def _hyena_deltas():
    min_decay = math.log(HYENA_DECAY_TARGET) / HYENA_DECAY_SLOW
    max_decay = math.log(HYENA_DECAY_TARGET) / HYENA_DECAY_FAST
    return jnp.abs(jnp.linspace(min_decay, max_decay, D_HYENA, dtype=F32))


def _pos_feats(length, grid):
    d = jnp.arange(length, dtype=jnp.int32)
    t = d.astype(F32) / length
    bands = jnp.arange(1, N_BANDS + 1, dtype=F32)
    ang = 2.0 * math.pi * t[:, None] * bands[None, :]
    feats = [t[:, None], jnp.sin(ang), jnp.cos(ang)]
    if grid:
        rows = max(length // GRID_W, 1)
        col_ang = 2.0 * math.pi * (d % GRID_W).astype(F32) / GRID_W
        row = (d // GRID_W).astype(F32) / rows
        feats += [jnp.sin(col_ang)[:, None], jnp.cos(col_ang)[:, None], row[:, None]]
    else:
        feats.append(jnp.zeros((length, 3), F32))
    z = jnp.concatenate(feats, axis=-1)
    return jnp.pad(z, ((0, 0), (0, LANES - POS_DIM)))


def _ada_kernel(cv_ref, w_ref, b_ref, o_ref):
    s = _silu(cv_ref[...])
    o_ref[...] = _dot(s, w_ref[...]) + b_ref[...]


def _ada_mods(cv, ada_w, ada_b):
    tn = D_MODEL
    return pl.pallas_call(
        _ada_kernel,
        out_shape=jax.ShapeDtypeStruct((DEPTH, 16, 6 * D_MODEL), F32),
        grid=(DEPTH, 6 * D_MODEL // tn),
        in_specs=[
            pl.BlockSpec((16, D_MODEL), lambda l, j: (0, 0)),
            pl.BlockSpec((None, D_MODEL, tn), lambda l, j: (l, 0, j)),
            pl.BlockSpec((None, 1, tn), lambda l, j: (l, 0, j)),
        ],
        out_specs=pl.BlockSpec((None, 16, tn), lambda l, j: (l, 0, j)),
        compiler_params=_cparams(("arbitrary", "arbitrary")),
        name="ada_mods",
    )(cv, ada_w, ada_b.reshape(DEPTH, 1, 6 * D_MODEL))


def _in_kernel(x_ref, prev_ref, next_ref, g_ref, mod_ref, w_ref, sw_ref, sb_ref, cw_ref, cb_ref,
               u_ref, x2_ref, xr_ref, pg_ref, *, batch, tm):
    i = pl.program_id(0)
    last = pl.num_programs(0) - 1
    lead = 2 * batch
    xe = jnp.concatenate([prev_ref[...], x_ref[...], next_ref[...]], axis=0)
    h = _modulate(_rms(xe, g_ref[...]), mod_ref[:, 0:D_MODEL], mod_ref[:, D_MODEL:2 * D_MODEL])
    h = jnp.concatenate([h[:lead] * (i > 0).astype(F32), h[lead:lead + tm],
                         h[lead + tm:] * (i < last).astype(F32)], axis=0)
    p = jnp.dot(h.astype(BF16), w_ref[...], preferred_element_type=F32)

    def tap(d, c0, c1):
        return p[lead + d * batch:lead + d * batch + tm, c0:c1]

    def hyena_branch(j):
        c0, c1 = j * D_HYENA, (j + 1) * D_HYENA
        acc = sb_ref[:, c0:c1] + tap(-1, c0, c1) * sw_ref[0:1, c0:c1]
        acc = acc + tap(0, c0, c1) * sw_ref[1:2, c0:c1]
        return acc + tap(1, c0, c1) * sw_ref[2:3, c0:c1]

    u_ref[...] = hyena_branch(0) * hyena_branch(1)
    x2_ref[...] = hyena_branch(2)
    c0, c1 = 3 * D_HYENA, 3 * D_HYENA + D_LRU
    acc = cb_ref[...] + tap(-2, c0, c1) * cw_ref[0:1, :]
    acc = acc + tap(-1, c0, c1) * cw_ref[1:2, :]
    acc = acc + tap(0, c0, c1) * cw_ref[2:3, :]
    xr_ref[...] = acc + tap(1, c0, c1) * cw_ref[3:4, :]
    pg_ref[...] = tap(0, c1, c1 + D_LRU)


def _in_proj_convs(x, norm_mix, mods, w_in_bf, short_w, short_b, conv_w, conv_b, l, seg, batch, tm=512):
    rows = x.shape[0]
    per_prev = tm // (2 * batch)
    per_next = tm // batch
    n_next = rows // batch
    out = jax.ShapeDtypeStruct((rows, D_HYENA), F32)
    return pl.pallas_call(
        functools.partial(_in_kernel, batch=batch, tm=tm),
        out_shape=(out, out, out, out),
        grid=(rows // tm,),
        in_specs=[
            pl.BlockSpec((tm, D_MODEL), lambda i: (i, 0)),
            pl.BlockSpec((2 * batch, D_MODEL), lambda i: (jnp.maximum(i * per_prev - 1, 0), 0)),
            pl.BlockSpec((batch, D_MODEL), lambda i: (jnp.minimum((i + 1) * per_next, n_next - 1), 0)),
            pl.BlockSpec((None, 1, D_MODEL), lambda i: (l, 0, 0)),
            pl.BlockSpec((None, SUBLANES, 2 * D_MODEL), lambda i: (l, seg, 0)),
            pl.BlockSpec((None, D_MODEL, N_IN), lambda i: (l, 0, 0)),
            pl.BlockSpec((None, 3, 3 * D_HYENA), lambda i: (l, 0, 0)),
            pl.BlockSpec((None, 1, 3 * D_HYENA), lambda i: (l, 0, 0)),
            pl.BlockSpec((None, 4, D_LRU), lambda i: (l, 0, 0)),
            pl.BlockSpec((None, 1, D_LRU), lambda i: (l, 0, 0)),
        ],
        out_specs=(pl.BlockSpec((tm, D_HYENA), lambda i: (i, 0)),) * 4,
        compiler_params=_cparams(("arbitrary",)),
        name="in_proj_convs",
    )(x, x, x, norm_mix.reshape(DEPTH, 1, D_MODEL), mods, w_in_bf,
      short_w, short_b.reshape(DEPTH, 1, 3 * D_HYENA), conv_w, conv_b.reshape(DEPTH, 1, D_LRU))


def _lru_gates(xr, w_ref, ba, bx, sp):
    xb = xr.astype(BF16)
    half = D_LRU // 2
    ra, rx = [], []
    for hf in range(2):
        xs = xb[:, hf * half:(hf + 1) * half]
        ra.append(jnp.dot(xs, w_ref[0, hf], preferred_element_type=F32))
        rx.append(jnp.dot(xs, w_ref[1, hf], preferred_element_type=F32))
    ta = jnp.tanh(jnp.concatenate(ra, axis=1) + ba)
    tx = jnp.tanh(jnp.concatenate(rx, axis=1) + bx)
    c1 = (0.5 * LRU_C) * sp
    neg_log_a = c1 + c1 * ta
    a = jnp.exp2(neg_log_a * (-1.0 / math.log(2.0)))
    mult = jnp.sqrt(jnp.maximum(jnp.tanh(neg_log_a) * (1.0 + a * a), 0.0))
    return a, (0.5 * xr) * (1.0 + tx) * mult


def _softplus(x):
    return jnp.maximum(x, 0.0) + jnp.log1p(jnp.exp(-jnp.abs(x)))


def _lru_kernel(*refs, batch, tm, chunk, reverse):
    if reverse:
        (xr_ref, w_ref, ba_ref, bx_ref, lam_ref, h0_ref, hf_ref, pg_ref,
         y_ref, hl_ref, a_scr, b_scr, h_scr) = refs
    else:
        (xr_ref, w_ref, ba_ref, bx_ref, lam_ref, h0_ref,
         y_ref, hl_ref, a_scr, b_scr, h_scr) = refs
    i = pl.program_id(0)

    @pl.when(i == 0)
    def _():
        h_scr[...] = h0_ref[...].astype(F32)

    sp = _softplus(-lam_ref[...])
    ba = 0.5 * ba_ref[...]
    bx = 0.5 * bx_ref[...]

    def fill(c, carry):
        r = pl.multiple_of(c * chunk, chunk)
        a, b = _lru_gates(xr_ref[pl.ds(r, chunk), :], w_ref, ba, bx, sp)
        a_scr[pl.ds(r, chunk), :] = a
        b_scr[pl.ds(r, chunk), :] = b
        return carry

    lax.fori_loop(0, tm // chunk, fill, 0)

    steps = tm // batch

    def step(s, h):
        t = (steps - 1 - s) if reverse else s
        r = pl.multiple_of(t * batch, batch)
        h = a_scr[pl.ds(r, batch), :] * h + b_scr[pl.ds(r, batch), :]
        if reverse:
            y_ref[pl.ds(r, batch), :] = ((hf_ref[pl.ds(r, batch), :] + h)
                                         * _gelu_tanh(pg_ref[pl.ds(r, batch), :]))
        else:
            y_ref[pl.ds(r, batch), :] = h
        return h

    h = lax.fori_loop(0, steps, step, h_scr[...], unroll=8)
    h_scr[...] = h

    @pl.when(i == pl.num_programs(0) - 1)
    def _():
        hl_ref[...] = h


def _lru_scan(xr, wbd, lru_ba, lru_bx, lru_lambda, h0, h0_block, l, direction, batch,
              hf=None, pg=None, tm=2048, chunk=256):
    rows = xr.shape[0]
    nt = rows // tm
    reverse = direction == 1
    row_map = (lambda i: (nt - 1 - i, 0)) if reverse else (lambda i: (i, 0))
    vec = lambda: pl.BlockSpec((None, None, 1, D_LRU), lambda i: (l, direction, 0, 0))
    in_specs = [
        pl.BlockSpec((tm, D_LRU), row_map),
        pl.BlockSpec((None, None, 2, 2, D_LRU // 2, D_LRU // 2), lambda i: (l, direction, 0, 0, 0, 0)),
        vec(), vec(), vec(),
        pl.BlockSpec((batch, D_LRU), lambda i: h0_block),
    ]
    args = [xr, wbd, lru_ba.reshape(DEPTH, 2, 1, D_LRU), lru_bx.reshape(DEPTH, 2, 1, D_LRU),
            lru_lambda.reshape(DEPTH, 2, 1, D_LRU), h0]
    if reverse:
        in_specs += [pl.BlockSpec((tm, D_LRU), row_map), pl.BlockSpec((tm, D_LRU), row_map)]
        args += [hf, pg]
    return pl.pallas_call(
        functools.partial(_lru_kernel, batch=batch, tm=tm, chunk=chunk, reverse=reverse),
        out_shape=(jax.ShapeDtypeStruct((rows, D_LRU), F32), jax.ShapeDtypeStruct((batch, D_LRU), F32)),
        grid=(nt,),
        in_specs=in_specs,
        out_specs=(pl.BlockSpec((tm, D_LRU), row_map), pl.BlockSpec((batch, D_LRU), lambda i: (0, 0))),
        scratch_shapes=[pltpu.VMEM((tm, D_LRU), F32), pltpu.VMEM((tm, D_LRU), F32),
                        pltpu.VMEM((batch, D_LRU), F32)],
        compiler_params=_cparams(("arbitrary",)),
        name="lru_rev" if reverse else "lru_fwd",
    )(*args)


def _filter_kernel(z_ref, w1_ref, b1_ref, w2_ref, b2_ref, w3_ref, b3_ref, fr_ref, dl_ref,
                   ks_ref, kd_ref, sum_ref):
    i = pl.program_id(0)
    hi = lax.Precision.HIGHEST
    z = z_ref[...]
    h1 = jnp.sin(fr_ref[0:1, :] * (jnp.dot(z, w1_ref[...], precision=hi, preferred_element_type=F32)
                                   + b1_ref[...]))
    h2 = jnp.sin(fr_ref[1:2, :] * (jnp.dot(h1, w2_ref[...], precision=hi, preferred_element_type=F32)
                                   + b2_ref[...]))
    k = jnp.dot(h2, w3_ref[...], precision=hi, preferred_element_type=F32) + b3_ref[...]
    window = jnp.exp(-z[:, 0:1] * dl_ref[...]) + HYENA_SHIFT
    kf = k[:, :D_HYENA] * window
    kb = k[:, D_HYENA:] * window
    row = lax.broadcasted_iota(jnp.int32, kb.shape, 0) + i * kb.shape[0]
    kb = jnp.where(row == 0, 0.0, kb)
    ks_ref[...] = kf + kb
    kd_ref[...] = kf - kb
    part = jnp.sum(jnp.abs(kf) + jnp.abs(kb), axis=0, keepdims=True)

    @pl.when(i == 0)
    def _():
        sum_ref[...] = part

    @pl.when(i > 0)
    def _():
        sum_ref[...] += part


def _hyena_filter_taps(length, grid, p, l, tl=256, n2_major=False):
    z = _pos_feats(length, grid)
    if n2_major:
        z = z.reshape(length // FFT_N2, FFT_N2, LANES).transpose(1, 0, 2).reshape(length, LANES)
    w1 = jnp.pad(p['filt_w1'], ((0, 0), (0, LANES - POS_DIM), (0, 0)))
    tap = jax.ShapeDtypeStruct((length, D_HYENA), F32)
    h = FILTER_HIDDEN
    return pl.pallas_call(
        _filter_kernel,
        out_shape=(tap, tap, jax.ShapeDtypeStruct((1, D_HYENA), F32)),
        grid=(length // tl,),
        in_specs=[
            pl.BlockSpec((tl, LANES), lambda i: (i, 0)),
            pl.BlockSpec((None, LANES, h), lambda i: (l, 0, 0)),
            pl.BlockSpec((None, 1, h), lambda i: (l, 0, 0)),
            pl.BlockSpec((None, h, h), lambda i: (l, 0, 0)),
            pl.BlockSpec((None, 1, h), lambda i: (l, 0, 0)),
            pl.BlockSpec((None, h, 2 * D_HYENA), lambda i: (l, 0, 0)),
            pl.BlockSpec((None, 1, 2 * D_HYENA), lambda i: (l, 0, 0)),
            pl.BlockSpec((None, 2, h), lambda i: (l, 0, 0)),
            pl.BlockSpec((1, D_HYENA), lambda i: (0, 0)),
        ],
        out_specs=(pl.BlockSpec((tl, D_HYENA), lambda i: (i, 0)),
                   pl.BlockSpec((tl, D_HYENA), lambda i: (i, 0)),
                   pl.BlockSpec((1, D_HYENA), lambda i: (0, 0))),
        compiler_params=_cparams(("arbitrary",)),
        name="hyena_filter",
    )(z, w1, p['filt_b1'].reshape(DEPTH, 1, h), p['filt_w2'], p['filt_b2'].reshape(DEPTH, 1, h),
      p['filt_w3'], p['filt_b3'].reshape(DEPTH, 1, 2 * D_HYENA), p['filt_freq'],
      _hyena_deltas().reshape(1, D_HYENA))


def _spec_direct_kernel(c_ref, s_ref, ks_ref, kd_ref, sum_ref, o_ref):
    scale = 1.0 / (sum_ref[...] + EPS)
    o_ref[0] = _dot(c_ref[...], ks_ref[...], x3=True) * scale
    o_ref[1] = _dot(s_ref[...], kd_ref[...], x3=True) * scale


def _filter_spectrum_direct(ks, kd, ksum):
    length = ks.shape[0]
    n = 2 * length
    _, _, c, ms = _dft_consts_direct(length)
    return pl.pallas_call(
        _spec_direct_kernel,
        out_shape=jax.ShapeDtypeStruct((2, n, D_HYENA), F32),
        compiler_params=_cparams(()),
        name="filter_spectrum_direct",
    )(jnp.asarray(c), jnp.asarray(ms), ks, kd, ksum)


def _conv_direct_kernel(u_ref, x2_ref, kh_ref, fwd_ref, inv_ref, bias_ref, o_ref, *, length, batch):
    n = 2 * length
    kr, ki = kh_ref[0], kh_ref[1]
    for q in range(batch // 2):
        z = jnp.concatenate([u_ref[:, 2 * q, :], u_ref[:, 2 * q + 1, :]], axis=0)
        x = _dot(fwd_ref[...], z, x3=True)
        xr, xi = x[:n], x[n:]
        y = jnp.concatenate([xr * kr - xi * ki, xr * ki + xi * kr], axis=0)
        conv = _dot(inv_ref[...], y, x3=True)
        o_ref[:, 2 * q, :] = conv[:length]
        o_ref[:, 2 * q + 1, :] = conv[length:]
    o_ref[...] = x2_ref[...] * (o_ref[...] + u_ref[...] * bias_ref[...])


def _hyena_conv_direct(u, x2, khat, hyena_bias, l, batch, cw=256):
    rows = u.shape[0]
    length = rows // batch
    fwd, inv, _, _ = _dft_consts_direct(length)
    blk = pl.BlockSpec((length, batch, cw), lambda j: (0, 0, j))
    out = pl.pallas_call(
        functools.partial(_conv_direct_kernel, length=length, batch=batch),
        out_shape=jax.ShapeDtypeStruct((length, batch, D_HYENA), F32),
        grid=(D_HYENA // cw,),
        in_specs=[blk, blk,
                  pl.BlockSpec((2, 2 * length, cw), lambda j: (0, 0, j)),
                  pl.BlockSpec(fwd.shape, lambda j: (0, 0)),
                  pl.BlockSpec(inv.shape, lambda j: (0, 0)),
                  pl.BlockSpec((None, 1, cw), lambda j: (l, 0, j))],
        out_specs=blk,
        compiler_params=_cparams(("arbitrary",)),
        name="hyena_conv_direct",
    )(u.reshape(length, batch, D_HYENA), x2.reshape(length, batch, D_HYENA), khat,
      jnp.asarray(fwd), jnp.asarray(inv), hyena_bias.reshape(DEPTH, 1, D_HYENA))
    return out.reshape(rows, D_HYENA)


def _stage1_kernel(x_ref, m_ref, o_ref, *, batch):
    m = m_ref[...].astype(BF16)
    for j in range(N2_BLOCK):
        for q in range(batch // 2):
            z = jnp.concatenate([x_ref[:, j, 2 * q, :], x_ref[:, j, 2 * q + 1, :]], axis=0)
            o_ref[j, q] = jnp.dot(m, z.astype(BF16), preferred_element_type=F32)


def _fft_stage1(u, batch, cw=256):
    c = u.shape[1]
    half1 = FFT_N1 // 2
    pairs = batch // 2
    m1, _, _, _ = _dft_consts()
    return pl.pallas_call(
        functools.partial(_stage1_kernel, batch=batch),
        out_shape=jax.ShapeDtypeStruct((FFT_N2, pairs, 2 * FFT_N1, c), F32),
        grid=(FFT_N2 // N2_BLOCK, c // cw),
        in_specs=[pl.BlockSpec((half1, N2_BLOCK, batch, cw), lambda j, k: (0, j, 0, k)),
                  pl.BlockSpec(m1.shape, lambda j, k: (0, 0))],
        out_specs=pl.BlockSpec((N2_BLOCK, pairs, 2 * FFT_N1, cw), lambda j, k: (j, 0, 0, k)),
        compiler_params=_cparams(("arbitrary", "arbitrary")),
        name="fft_stage1",
    )(u.reshape(half1, FFT_N2, batch, c), jnp.asarray(m1))


def _stage1_filter_kernel(ks_ref, kd_ref, m_ref, o_ref):
    m = m_ref[:, :FFT_N1 // 2]
    for j in range(N2_BLOCK):
        for g, ref in enumerate((ks_ref, kd_ref)):
            o_ref[j, g] = _dot(m, ref[j], x3=True)


def _stage2_filter_kernel(a_ref, g_ref, sum_ref, o_ref):
    scale = 1.0 / (sum_ref[...] + EPS)
    res = []
    for j in range(K1_BLOCK):
        bs = jnp.concatenate([a_ref[:, 0, 0, j, :], a_ref[:, 0, 1, j, :]], axis=0)
        bd = jnp.concatenate([a_ref[:, 1, 0, j, :], a_ref[:, 1, 1, j, :]], axis=0)
        res.append((_dot(g_ref[j], bs, x3=True), _dot(g_ref[j], bd, x3=True)))
    for j in range(K1_BLOCK):
        o_ref[0, j] = res[j][0][:FFT_N2] * scale
        o_ref[1, j] = res[j][1][FFT_N2:] * scale


def _filter_spectrum_2stage(ks, kd, ksum):
    c = D_HYENA
    half1 = FFT_N1 // 2
    m1, g2, _, _ = _dft_consts()
    tap = pl.BlockSpec((N2_BLOCK, half1, c), lambda j: (j, 0, 0))
    a = pl.pallas_call(
        _stage1_filter_kernel,
        out_shape=jax.ShapeDtypeStruct((FFT_N2, 2, 2 * FFT_N1, c), F32),
        grid=(FFT_N2 // N2_BLOCK,),
        in_specs=[tap, tap, pl.BlockSpec(m1.shape, lambda j: (0, 0))],
        out_specs=pl.BlockSpec((N2_BLOCK, 2, 2 * FFT_N1, c), lambda j: (j, 0, 0, 0)),
        compiler_params=_cparams(("arbitrary",)),
        name="filter_spectrum_stage1",
    )(ks.reshape(FFT_N2, half1, c), kd.reshape(FFT_N2, half1, c), jnp.asarray(m1))
    return pl.pallas_call(
        _stage2_filter_kernel,
        out_shape=jax.ShapeDtypeStruct((2, FFT_N1, FFT_N2, c), F32),
        grid=(FFT_N1 // K1_BLOCK,),
        in_specs=[pl.BlockSpec((FFT_N2, 2, 2, K1_BLOCK, c), lambda i: (0, 0, 0, i, 0)),
                  pl.BlockSpec((K1_BLOCK, 2 * FFT_N2, 2 * FFT_N2), lambda i: (i, 0, 0)),
                  pl.BlockSpec((1, c), lambda i: (0, 0))],
        out_specs=pl.BlockSpec((2, K1_BLOCK, FFT_N2, c), lambda i: (0, i, 0, 0)),
        compiler_params=_cparams(("arbitrary",)),
        name="filter_spectrum_stage2",
    )(a.reshape(FFT_N2, 2, 2, FFT_N1, c), jnp.asarray(g2), ksum)


def _stage2_kernel(a_ref, kh_ref, g_ref, gt_ref, o_ref):
    xs = []
    for j in range(K1_BLOCK):
        b = jnp.concatenate([a_ref[:, 0, j, :], a_ref[:, 1, j, :]], axis=0)
        xs.append(_dot(g_ref[j], b))
    ys = []
    for j in range(K1_BLOCK):
        xr, xi = xs[j][:FFT_N2], xs[j][FFT_N2:]
        kr, ki = kh_ref[0, j], kh_ref[1, j]
        ys.append(jnp.concatenate([xr * kr - xi * ki, xr * ki + xi * kr], axis=0).astype(BF16))
    for j in range(K1_BLOCK):
        o_ref[j] = jnp.dot(gt_ref[j].astype(BF16), ys[j], preferred_element_type=F32)


def _fft_stage2(a, khat):
    _, g2, g2t, _ = _dft_consts()
    c = D_HYENA
    pairs = a.shape[1]
    gblk = pl.BlockSpec((K1_BLOCK, 2 * FFT_N2, 2 * FFT_N2), lambda i, q: (i, 0, 0))
    return pl.pallas_call(
        _stage2_kernel,
        out_shape=jax.ShapeDtypeStruct((FFT_N1, pairs, 2 * FFT_N2, c), F32),
        grid=(FFT_N1 // K1_BLOCK, pairs),
        in_specs=[pl.BlockSpec((FFT_N2, None, 2, K1_BLOCK, c), lambda i, q: (0, q, 0, i, 0)),
                  pl.BlockSpec((2, K1_BLOCK, FFT_N2, c), lambda i, q: (0, i, 0, 0)), gblk, gblk],
        out_specs=pl.BlockSpec((K1_BLOCK, None, 2 * FFT_N2, c), lambda i, q: (i, q, 0, 0)),
        compiler_params=_cparams(("arbitrary", "arbitrary")),
        name="fft_stage2",
    )(a.reshape(FFT_N2, pairs, 2, FFT_N1, c), khat, jnp.asarray(g2), jnp.asarray(g2t))


def _stage3_kernel(a_ref, m_ref, u_ref, x2_ref, bias_ref, o_ref, *, batch):
    half1 = FFT_N1 // 2
    m = m_ref[...].astype(BF16)
    for j in range(N2_BLOCK):
        b = jnp.concatenate([a_ref[:, q, ri, j, :] for q in range(batch // 2) for ri in range(2)], axis=0)
        y = jnp.dot(m, b.astype(BF16), preferred_element_type=F32)
        o_ref[:, j] = y.reshape(half1, batch, y.shape[-1])
    o_ref[...] = x2_ref[...] * (o_ref[...] + u_ref[...] * bias_ref[...])


def _fft_stage3(a, u, x2, hyena_bias, l, batch, cw=256):
    rows, c = u.shape
    half1 = FFT_N1 // 2
    pairs = batch // 2
    m3 = _inverse_stage1_interleaved(batch)
    blk = pl.BlockSpec((half1, N2_BLOCK, batch, cw), lambda j, k: (0, j, 0, k))
    out = pl.pallas_call(
        functools.partial(_stage3_kernel, batch=batch),
        out_shape=jax.ShapeDtypeStruct((half1, FFT_N2, batch, c), F32),
        grid=(FFT_N2 // N2_BLOCK, c // cw),
        in_specs=[pl.BlockSpec((FFT_N1, pairs, 2, N2_BLOCK, cw), lambda j, k: (0, 0, 0, j, k)),
                  pl.BlockSpec(m3.shape, lambda j, k: (0, 0)),
                  blk, blk,
                  pl.BlockSpec((None, 1, cw), lambda j, k: (l, 0, k))],
        out_specs=blk,
        compiler_params=_cparams(("arbitrary", "arbitrary")),
        name="fft_stage3",
    )(a.reshape(FFT_N1, pairs, 2, FFT_N2, c), jnp.asarray(m3), u.reshape(half1, FFT_N2, batch, c),
      x2.reshape(half1, FFT_N2, batch, c), hyena_bias.reshape(DEPTH, 1, c))
    return out.reshape(rows, c)


def _hyena_conv_2stage(u, x2, khat, hyena_bias, l, batch):
    a = _fft_stage1(u, batch)
    a = _fft_stage2(a, khat)
    return _fft_stage3(a, u, x2, hyena_bias, l, batch)


def _mix_and_prenorm(hy_ref, lr_ref, x_ref, w_ref, g_ref, mod_ref):
    d = D_MODEL
    mixed = (jnp.dot(hy_ref[...].astype(BF16), w_ref[0:D_HYENA, :], preferred_element_type=F32)
             + jnp.dot(lr_ref[...].astype(BF16), w_ref[D_HYENA:, :], preferred_element_type=F32))
    rows = mixed.shape[0]
    g1 = mod_ref[:, 2 * d:3 * d]
    x = x_ref[...] + (mixed.reshape(rows // SUBLANES, SUBLANES, d) * g1[None]).reshape(rows, d)
    h = _modulate(_rms(x, g_ref[...]), mod_ref[:, 3 * d:4 * d], mod_ref[:, 4 * d:5 * d])
    return x, h.astype(BF16)


def _out_kernel(hy_ref, lr_ref, x_ref, w_ref, g_ref, mod_ref, xo_ref, h_ref):
    xo_ref[...], h_ref[...] = _mix_and_prenorm(hy_ref, lr_ref, x_ref, w_ref, g_ref, mod_ref)


def _out_proj(hy, lr, x, w_out_bf, norm_ffn, mods, l, seg, tm=512):
    rows = x.shape[0]
    d = D_MODEL
    return pl.pallas_call(
        _out_kernel,
        out_shape=(jax.ShapeDtypeStruct((rows, d), F32), jax.ShapeDtypeStruct((rows, d), BF16)),
        grid=(rows // tm,),
        in_specs=[
            pl.BlockSpec((tm, D_HYENA), lambda i: (i, 0)),
            pl.BlockSpec((tm, D_LRU), lambda i: (i, 0)),
            pl.BlockSpec((tm, d), lambda i: (i, 0)),
            pl.BlockSpec((None, D_HYENA + D_LRU, d), lambda i: (l, 0, 0)),
            pl.BlockSpec((None, 1, d), lambda i: (l, 0, 0)),
            pl.BlockSpec((None, SUBLANES, 6 * d), lambda i: (l, seg, 0)),
        ],
        out_specs=(pl.BlockSpec((tm, d), lambda i: (i, 0)), pl.BlockSpec((tm, d), lambda i: (i, 0))),
        compiler_params=_cparams(("arbitrary",)),
        name="out_proj",
    )(hy, lr, x, w_out_bf, norm_ffn.reshape(DEPTH, 1, d), mods)


def _gate_residual(x, acc, g2):
    rows, d = acc.shape
    return x + (acc.reshape(rows // SUBLANES, SUBLANES, d) * g2[None]).reshape(rows, d)


def _swiglu_chunk(h, wgu_ref, wd_ref):
    gu = jnp.dot(h, wgu_ref[...], preferred_element_type=F32)
    fc = gu.shape[1] // 2
    return jnp.dot((_silu(gu[:, :fc]) * gu[:, fc:]).astype(BF16), wd_ref[...], preferred_element_type=F32)


def _chunked_gate_up(w_gu, fc):
    lead = w_gu.shape[:-1]
    nf = D_FF // fc
    gate = w_gu[..., :D_FF].reshape(*lead, nf, fc)
    up = w_gu[..., D_FF:].reshape(*lead, nf, fc)
    return jnp.concatenate([gate, up], axis=-1).reshape(*lead, 2 * D_FF)


def _mix_ffn_kernel(hy_ref, lr_ref, x_ref, w_ref, g_ref, mod_ref, wgu_ref, wd_ref,
                    o_ref, x_scr, h_scr, acc_ref):
    f = pl.program_id(1)

    @pl.when(f == 0)
    def _():
        x_scr[...], h_scr[...] = _mix_and_prenorm(hy_ref, lr_ref, x_ref, w_ref, g_ref, mod_ref)

    part = _swiglu_chunk(h_scr[...], wgu_ref, wd_ref)

    @pl.when(f == 0)
    def _():
        acc_ref[...] = part

    @pl.when(f > 0)
    def _():
        acc_ref[...] += part

    @pl.when(f == pl.num_programs(1) - 1)
    def _():
        o_ref[...] = _gate_residual(x_scr[...], acc_ref[...], mod_ref[:, 5 * D_MODEL:])


def _out_proj_ffn_dense(hy, lr, x, w_out_bf, norm_ffn, w_gu_bf, w_down_bf, mods, m, l, seg, tm=512, fc=FFN_CHUNK):
    rows = x.shape[0]
    d = D_MODEL
    nf = D_FF // fc
    return pl.pallas_call(
        _mix_ffn_kernel,
        out_shape=jax.ShapeDtypeStruct((rows, d), F32),
        grid=(rows // tm, nf),
        in_specs=[
            pl.BlockSpec((tm, D_HYENA), lambda i, f: (i, 0)),
            pl.BlockSpec((tm, D_LRU), lambda i, f: (i, 0)),
            pl.BlockSpec((tm, d), lambda i, f: (i, 0)),
            pl.BlockSpec((None, D_HYENA + D_LRU, d), lambda i, f: (l, 0, 0)),
            pl.BlockSpec((None, 1, d), lambda i, f: (l, 0, 0)),
            pl.BlockSpec((None, SUBLANES, 6 * d), lambda i, f: (l, seg, 0)),
            pl.BlockSpec((None, d, 2 * fc), lambda i, f: (m, 0, f)),
            pl.BlockSpec((None, fc, d), lambda i, f: (m, f, 0)),
        ],
        out_specs=pl.BlockSpec((tm, d), lambda i, f: (i, 0)),
        scratch_shapes=[pltpu.VMEM((tm, d), F32), pltpu.VMEM((tm, d), BF16), pltpu.VMEM((tm, d), F32)],
        compiler_params=_cparams(("arbitrary", "arbitrary")),
        name="out_proj_ffn_dense",
    )(hy, lr, x, w_out_bf, norm_ffn.reshape(DEPTH, 1, d), mods, w_gu_bf, w_down_bf)


def _route_top2(logits):
    lane = lax.broadcasted_iota(jnp.int32, logits.shape, 1)
    neg = jnp.float32(-jnp.inf)
    v = jnp.where(lane < N_EXPERTS, logits, neg)
    m1 = jnp.max(v, axis=-1, keepdims=True)
    i1 = jnp.min(jnp.where(v == m1, lane, LANES), axis=-1, keepdims=True)
    v2 = jnp.where(lane == i1, neg, v)
    m2 = jnp.max(v2, axis=-1, keepdims=True)
    i2 = jnp.min(jnp.where(v2 == m2, lane, LANES), axis=-1, keepdims=True)
    e2 = jnp.exp(m2 - m1)
    w1 = 1.0 / (1.0 + e2)
    w2 = e2 / (1.0 + e2)
    first = lane == i1
    second = lane == i2
    comb = jnp.where(first, w1, 0.0) + jnp.where(second, w2, 0.0)
    return comb, jnp.where(first | second, 1.0, 0.0)


MOE_TILE = 2048
MOE_CH = 256
MOE_SB = 128
MOE_WIN = 256
MOE_CS_ROWS = 16


def _router_kernel(h_ref, r_ref, comb_ref, key_ref, keyt_ref, cs_ref, *, tt):
    nch = tt // MOE_CH
    r_hi, r_lo = _split(r_ref[...])
    h = h_ref[...]
    logits = (jnp.dot(h, r_hi, preferred_element_type=F32) + jnp.dot(h, r_lo, preferred_element_type=F32))
    comb, sel = _route_top2(logits)
    comb_ref[...] = comb
    rr = lax.broadcasted_iota(jnp.int32, (MOE_CH, MOE_CH), 0)
    cc = lax.broadcasted_iota(jnp.int32, (MOE_CH, MOE_CH), 1)
    strict_lower = jnp.where(cc < rr, 1.0, 0.0).astype(BF16)
    cs_ref[...] = jnp.zeros(cs_ref.shape, jnp.int32)
    carry = jnp.zeros((1, LANES), F32)
    for c in range(nch):
        selc = sel[c * MOE_CH:(c + 1) * MOE_CH]
        cs_ref[c:c + 1, :] = carry.astype(jnp.int32)
        rank = jnp.dot(strict_lower, selc.astype(BF16), preferred_element_type=F32) + carry
        key_ref[c * MOE_CH:(c + 1) * MOE_CH, :] = jnp.where(selc > 0.0, rank, -1.0)
        carry = carry + jnp.sum(selc, axis=0, keepdims=True)
    cs_ref[nch:nch + 1, :] = carry.astype(jnp.int32)
    keyt = jnp.transpose(key_ref[...])
    for e in range(N_EXPERTS):
        for c in range(nch):
            keyt_ref[e, c] = keyt[e:e + 1, c * MOE_CH:(c + 1) * MOE_CH]


def _moe_route(h, router_pad, m, tt):
    rows, d = h.shape
    nt = rows // tt
    nch = tt // MOE_CH
    tok = jax.ShapeDtypeStruct((rows, LANES), F32)
    return pl.pallas_call(
        functools.partial(_router_kernel, tt=tt),
        out_shape=(tok, tok,
                   jax.ShapeDtypeStruct((N_EXPERTS, rows // MOE_CH, 1, MOE_CH), F32),
                   jax.ShapeDtypeStruct((nt * MOE_CS_ROWS, LANES), jnp.int32)),
        grid=(nt,),
        in_specs=[pl.BlockSpec((tt, d), lambda i: (i, 0)),
                  pl.BlockSpec((None, d, LANES), lambda i: (m, 0, 0))],
        out_specs=(pl.BlockSpec((tt, LANES), lambda i: (i, 0)),
                   pl.BlockSpec((tt, LANES), lambda i: (i, 0)),
                   pl.BlockSpec((N_EXPERTS, nch, 1, MOE_CH), lambda i: (0, i, 0, 0)),
                   pl.BlockSpec((MOE_CS_ROWS, LANES), lambda i: (i, 0))),
        compiler_params=_cparams(("arbitrary",)),
        name="moe_route",
    )(h, router_pad)


def _moe_kernel(cs_ref, h_ref, x_ref, comb_ref, key_ref, keyt_ref, wgu_ref, wd_ref, mod_ref,
                o_ref, xg_ref, y_ref, *, tt):
    i = pl.program_id(0)
    e = pl.program_id(1)
    f = pl.program_id(2)
    last_f = pl.num_programs(2) - 1
    nch = tt // MOE_CH
    base = i * ((nch + 1) * N_EXPERTS) + e

    def count_before(c):
        return cs_ref[base + c * N_EXPERTS]

    n_sub = (count_before(nch) + (MOE_SB - 1)) // MOE_SB

    def windows(c):
        lo = count_before(c)
        hi = count_before(c + 1)
        w0 = (lo // MOE_SB) * MOE_SB
        return w0, (hi - w0 + (MOE_WIN - 1)) // MOE_WIN

    def row_block(r0, size):
        return pl.ds(pl.multiple_of(r0, MOE_SB), size)

    @pl.when((e == 0) & (f == 0))
    def _():
        o_ref[...] = jnp.zeros_like(o_ref)

    @pl.when(f == 0)
    def _():
        def clear(j, carry):
            xg_ref[row_block(j * MOE_SB, MOE_SB), :] = jnp.zeros((MOE_SB, D_MODEL), BF16)
            return carry

        lax.fori_loop(0, n_sub + MOE_WIN // MOE_SB, clear, 0)
        y_ref[row_block(n_sub * MOE_SB, MOE_WIN), :] = jnp.zeros((MOE_WIN, D_MODEL), BF16)

        def gather(c, r0):
            row = (lax.broadcasted_iota(jnp.int32, (MOE_WIN, MOE_CH), 0) + r0).astype(F32)
            sel = jnp.where(keyt_ref[e, c] == row, 1.0, 0.0).astype(BF16)
            part = jnp.dot(sel, h_ref[c * MOE_CH:(c + 1) * MOE_CH, :], preferred_element_type=F32)
            xg_ref[row_block(r0, MOE_WIN), :] += part.astype(BF16)

        starts = [windows(c) for c in range(nch)]
        for c in range(nch):
            gather(c, starts[c][0])
        for c in range(nch):
            w0, n_win = starts[c]
            lax.fori_loop(1, n_win, lambda k, carry, c=c, w0=w0: (gather(c, w0 + k * MOE_WIN), carry)[1], 0)

    def expert_rows(r0, size):
        rows = row_block(r0, size)
        xg = xg_ref[rows, :]
        y = _swiglu_chunk(xg, wgu_ref, wd_ref)

        @pl.when(f == 0)
        def _():
            y_ref[rows, :] = y.astype(BF16)

        @pl.when(f > 0)
        def _():
            y_ref[rows, :] = (y_ref[rows, :].astype(F32) + y).astype(BF16)

    n_pair = n_sub // 2
    lax.fori_loop(0, n_pair, lambda j, carry: (expert_rows(j * (2 * MOE_SB), 2 * MOE_SB), carry)[1], 0)

    @pl.when(n_sub % 2 == 1)
    def _():
        expert_rows(n_pair * (2 * MOE_SB), MOE_SB)

    @pl.when(f == last_f)
    def _():
        lane_e = lax.broadcasted_iota(jnp.int32, (MOE_CH, LANES), 1) == e

        def scatter(c, r0, key, w):
            pos = (lax.broadcasted_iota(jnp.int32, (MOE_CH, MOE_WIN), 1) + r0).astype(F32)
            sel = jnp.where(key == pos, 1.0, 0.0).astype(BF16)
            part = jnp.dot(sel, y_ref[row_block(r0, MOE_WIN), :], preferred_element_type=F32)
            o_ref[c * MOE_CH:(c + 1) * MOE_CH, :] += w * part

        starts = [windows(c) for c in range(nch)]
        cols = []
        for c in range(nch):
            tok = slice(c * MOE_CH, (c + 1) * MOE_CH)
            key = jnp.sum(jnp.where(lane_e, key_ref[tok, :], 0.0), axis=-1, keepdims=True)
            w = jnp.sum(jnp.where(lane_e, comb_ref[tok, :], 0.0), axis=-1, keepdims=True)
            cols.append((key, w))
            scatter(c, starts[c][0], key, w)
        for c in range(nch):
            w0, n_win = starts[c]
            key, w = cols[c]
            lax.fori_loop(1, n_win,
                          lambda k, carry, c=c, w0=w0, key=key, w=w:
                          (scatter(c, w0 + k * MOE_WIN, key, w), carry)[1], 0)

    @pl.when((e == pl.num_programs(1) - 1) & (f == last_f))
    def _():
        o_ref[...] = _gate_residual(x_ref[...], o_ref[...], mod_ref[...])


def _ffn_moe(h, x, router_pad, w_gu_bf, w_down_bf, mods, m, l, seg, fc=FFN_CHUNK):
    rows = x.shape[0]
    d = D_MODEL
    nf = D_FF // fc
    tt = MOE_TILE
    nt = rows // tt
    nch = tt // MOE_CH
    comb, key, keyt, cs = _moe_route(h, router_pad, m, tt)
    counts = cs.reshape(nt, MOE_CS_ROWS, LANES)[:, :nch + 1, :N_EXPERTS].reshape(-1)
    once = pl.Buffered(1)
    grid_spec = pltpu.PrefetchScalarGridSpec(
        num_scalar_prefetch=1,
        grid=(nt, N_EXPERTS, nf),
        in_specs=[
            pl.BlockSpec((tt, d), lambda i, e, f, cs: (i, 0), pipeline_mode=once),
            pl.BlockSpec((tt, d), lambda i, e, f, cs: (i, 0), pipeline_mode=once),
            pl.BlockSpec((tt, LANES), lambda i, e, f, cs: (i, 0), pipeline_mode=once),
            pl.BlockSpec((tt, LANES), lambda i, e, f, cs: (i, 0), pipeline_mode=once),
            pl.BlockSpec((N_EXPERTS, nch, 1, MOE_CH), lambda i, e, f, cs: (0, i, 0, 0), pipeline_mode=once),
            pl.BlockSpec((None, None, d, 2 * fc), lambda i, e, f, cs: (m, e, 0, f)),
            pl.BlockSpec((None, None, fc, d), lambda i, e, f, cs: (m, e, f, 0)),
            pl.BlockSpec((None, SUBLANES, d), lambda i, e, f, cs: (l, seg, 5)),
        ],
        out_specs=pl.BlockSpec((tt, d), lambda i, e, f, cs: (i, 0), pipeline_mode=once),
        scratch_shapes=[pltpu.VMEM((tt + MOE_SB + MOE_WIN, d), BF16),
                        pltpu.VMEM((tt + MOE_SB + MOE_WIN, d), BF16)],
    )
    return pl.pallas_call(
        functools.partial(_moe_kernel, tt=tt),
        out_shape=jax.ShapeDtypeStruct((rows, d), F32),
        grid_spec=grid_spec,
        compiler_params=_cparams(("arbitrary", "arbitrary", "arbitrary")),
        name="ffn_moe",
    )(counts, h, x, comb, key, keyt, w_gu_bf, w_down_bf, mods)


def _final_kernel(x_ref, g_ref, o_ref, *, batch):
    for b in range(batch):
        o_ref[b] = _rms(x_ref[:, b, :], g_ref[...])


def _final_norm(x, norm_final, batch, tl=64):
    rows, d = x.shape
    length = rows // batch
    return pl.pallas_call(
        functools.partial(_final_kernel, batch=batch),
        out_shape=jax.ShapeDtypeStruct((batch, length, d), F32),
        grid=(length // tl,),
        in_specs=[pl.BlockSpec((tl, batch, d), lambda i: (i, 0, 0)),
                  pl.BlockSpec((1, d), lambda i: (0, 0))],
        out_specs=pl.BlockSpec((batch, tl, d), lambda i: (0, i, 0)),
        compiler_params=_cparams(("arbitrary",)),
        name="final_norm",
    )(x.reshape(length, batch, d), norm_final.reshape(1, d))


def _time_major_kernel(x_ref, o_ref, *, batch):
    for b in range(batch):
        o_ref[:, b, :] = x_ref[b]


def _to_time_major(x, tl=64):
    batch, length, d = x.shape
    out = pl.pallas_call(
        functools.partial(_time_major_kernel, batch=batch),
        out_shape=jax.ShapeDtypeStruct((length, batch, d), x.dtype),
        grid=(length // tl,),
        in_specs=[pl.BlockSpec((batch, tl, d), lambda i: (0, i, 0))],
        out_specs=pl.BlockSpec((tl, batch, d), lambda i: (i, 0, 0)),
        compiler_params=_cparams(("arbitrary",)),
        name="to_time_major",
    )(x)
    return out.reshape(length * batch, d)


def _block_diag_heads(w):
    hh = N_LRU_HEADS // 2
    w = w.reshape(DEPTH, 2, 2, hh, LRU_HEAD_DIM, LRU_HEAD_DIM)
    eye = jnp.eye(hh, dtype=w.dtype)
    bd = jnp.einsum('ldgpij,pq->ldgpiqj', w, eye)
    return bd.reshape(DEPTH, 2, 2, hh * LRU_HEAD_DIM, hh * LRU_HEAD_DIM)


def kernel(x_prompt, x_sample, state_lru, c, c_ctx, norm_mix, norm_ffn, norm_final, ada_w, ada_b,
           w_in, w_out, hyena_short_w, hyena_short_b, filt_w1, filt_b1, filt_w2, filt_b2, filt_w3,
           filt_b3, filt_freq, hyena_bias, lru_conv_w, lru_conv_b, lru_wa, lru_ba, lru_wx, lru_bx,
           lru_lambda, ffn_w_gu, ffn_w_down, moe_router, moe_w_gu, moe_w_down):
    bp, lp, d = x_prompt.shape
    bs, ls, _ = x_sample.shape
    assert bs == SUBLANES and bp % SUBLANES == 0 and ls == (FFT_N1 // 2) * FFT_N2

    xs = _to_time_major(x_sample)
    xp = _to_time_major(x_prompt)

    cv = jnp.concatenate([c, jnp.broadcast_to(c_ctx[None, :], (SUBLANES, d))], axis=0)
    mods = _ada_mods(cv, ada_w, ada_b)

    w_in_bf = w_in.astype(BF16)
    w_out_bf = w_out.astype(BF16)
    ffn_gu_bf = _chunked_gate_up(ffn_w_gu, FFN_CHUNK).astype(BF16)
    ffn_down_bf = ffn_w_down.astype(BF16)
    moe_gu_bf = _chunked_gate_up(moe_w_gu, FFN_CHUNK).astype(BF16)
    moe_down_bf = moe_w_down.astype(BF16)
    router_pad = jnp.pad(moe_router, ((0, 0), (0, 0), (0, LANES - N_EXPERTS)))
    wbd = (0.5 * jnp.stack([_block_diag_heads(lru_wa), _block_diag_heads(lru_wx)], axis=2)).astype(BF16)
    filt = dict(filt_w1=filt_w1, filt_b1=filt_b1, filt_w2=filt_w2, filt_b2=filt_b2,
                filt_w3=filt_w3, filt_b3=filt_b3, filt_freq=filt_freq)
    state_flat = state_lru.reshape(bs, DEPTH * 2 * D_LRU)
    zero_state = jnp.zeros((bp, D_LRU), F32)

    new_states = []
    for l in range(DEPTH):
        m = l // 2
        for path in range(2):
            if path == 0:
                x, batch, seg, grid_pos = xp, bp, 1, False
            else:
                x, batch, seg, grid_pos = xs, bs, 0, True
            length = x.shape[0] // batch

            u, x2, xr, pg = _in_proj_convs(x, norm_mix, mods, w_in_bf, hyena_short_w, hyena_short_b,
                                           lru_conv_w, lru_conv_b, l, seg, batch)

            ks, kd, ksum = _hyena_filter_taps(length, grid_pos, filt, l, n2_major=(path == 1))
            if path == 0:
                khat = _filter_spectrum_direct(ks, kd, ksum)
                hy = _hyena_conv_direct(u, x2, khat, hyena_bias, l, batch)
            else:
                khat = _filter_spectrum_2stage(ks, kd, ksum)
                hy = _hyena_conv_2stage(u, x2, khat, hyena_bias, l, batch)

            lru_tm = min(2048, x.shape[0])
            if path == 0:
                h0f, h0f_blk, h0r, h0r_blk = zero_state, (0, 0), zero_state, (0, 0)
            else:
                h0f, h0f_blk, h0r, h0r_blk = state_flat, (0, 2 * l), state_flat, (0, 2 * l + 1)
            hf, hlast_f = _lru_scan(xr, wbd, lru_ba, lru_bx, lru_lambda, h0f, h0f_blk, l, 0, batch,
                                    tm=lru_tm)
            lr, hlast_r = _lru_scan(xr, wbd, lru_ba, lru_bx, lru_lambda, h0r, h0r_blk, l, 1, batch,
                                    hf=hf, pg=pg, tm=lru_tm)
            if path == 0:
                new_states.append(jnp.stack([hlast_f, hlast_r], axis=1))

            if l % 2 == 0:
                x = _out_proj_ffn_dense(hy, lr, x, w_out_bf, norm_ffn, ffn_gu_bf, ffn_down_bf, mods, m, l, seg)
            else:
                x, h = _out_proj(hy, lr, x, w_out_bf, norm_ffn, mods, l, seg)
                x = _ffn_moe(h, x, router_pad, moe_gu_bf, moe_down_bf, mods, m, l, seg)
            if path == 0:
                xp = x
            else:
                xs = x

    new_state_lru = jnp.stack(new_states, axis=1).astype(state_lru.dtype)
    y_prompt = _final_norm(xp, norm_final, bp)
    y_sample = _final_norm(xs, norm_final, bs)
    return (y_prompt, y_sample, new_state_lru)
```

```python
import functools
import math

import numpy as np
import jax
import jax.numpy as jnp
from jax import lax
from jax.experimental import pallas as pl
from jax.experimental.pallas import tpu as pltpu

F32 = jnp.float32
BF16 = jnp.bfloat16

D_MODEL = 1024
DEPTH = 4
D_HYENA = 512
D_LRU = 512
N_IN = 3 * D_HYENA + 2 * D_LRU
N_LRU_HEADS = 8
LRU_HEAD_DIM = D_LRU // N_LRU_HEADS
LRU_C = 8.0
FILTER_HIDDEN = 64
N_BANDS = 4
POS_DIM = 1 + 2 * N_BANDS + 3
GRID_W = 64
HYENA_DECAY_FAST = 0.3
HYENA_DECAY_SLOW = 1.5
HYENA_DECAY_TARGET = 1e-2
HYENA_SHIFT = 0.05
D_FF = 2816
N_EXPERTS = 8
EPS = 1e-6

SUBLANES = 8
LANES = 128
VMEM_LIMIT = 56 * 1024 * 1024

FFT_N1 = 128
FFT_N2 = 64
K1_BLOCK = 16
N2_BLOCK = 8
FFN_CHUNK = D_FF // 2


def _cparams(sem):
    return pltpu.CompilerParams(dimension_semantics=sem, vmem_limit_bytes=VMEM_LIMIT)


def _split(a):
    hi = a.astype(BF16)
    lo = (a - hi.astype(F32)).astype(BF16)
    return hi, lo


def _dot(a, b, x3=False):
    if not x3:
        return jnp.dot(a.astype(BF16), b.astype(BF16), preferred_element_type=F32)
    ah, al = _split(a)
    bh, bl = _split(b)
    return (jnp.dot(ah, bh, preferred_element_type=F32)
            + jnp.dot(ah, bl, preferred_element_type=F32)
            + jnp.dot(al, bh, preferred_element_type=F32))


def _silu(x):
    return x * jax.nn.sigmoid(x)


def _gelu_tanh(x):
    return 0.5 * x * (1.0 + jnp.tanh(math.sqrt(2.0 / math.pi) * (x + 0.044715 * (x * x * x))))


def _rms(x, g):
    return x * lax.rsqrt(jnp.mean(x * x, axis=-1, keepdims=True) + EPS) * g


def _modulate(y, shift, scale):
    rows, d = y.shape
    y3 = y.reshape(rows // SUBLANES, SUBLANES, d)
    return (y3 * (1.0 + scale)[None] + shift[None]).reshape(rows, d)


@functools.lru_cache(maxsize=None)
def _dft_consts():
    n = FFT_N1 * FFT_N2
    half1 = FFT_N1 // 2
    k1 = np.arange(FFT_N1)[:, None]
    n1 = np.arange(half1)[None, :]
    ang = 2.0 * np.pi * k1 * n1 / FFT_N1
    c, s = np.cos(ang), np.sin(ang)
    m1 = np.block([[c, s], [-s, c]])
    k2 = np.arange(FFT_N2)[None, :, None]
    n2 = np.arange(FFT_N2)[None, None, :]
    kk1 = np.arange(FFT_N1)[:, None, None]
    th = 2.0 * np.pi * (n2 * k2 / FFT_N2 + n2 * kk1 / n)
    fr, fi = np.cos(th), -np.sin(th)
    g2 = np.concatenate([np.concatenate([fr, -fi], axis=2), np.concatenate([fi, fr], axis=2)], axis=1)
    g2t = np.transpose(g2, (0, 2, 1))
    cm, sm = c.T / n, s.T / n
    m3 = np.block([[cm, -sm], [sm, cm]])
    f32 = lambda a: np.asarray(a, np.float32)
    return f32(m1), f32(g2), f32(g2t), f32(m3)


@functools.lru_cache(maxsize=None)
def _inverse_stage1_interleaved(batch):
    _, _, _, m3 = _dft_consts()
    half1 = FFT_N1 // 2
    pairs = batch // 2
    big = np.zeros((half1, batch, pairs, 2 * FFT_N1), np.float32)
    for q in range(pairs):
        for part in range(2):
            big[:, 2 * q + part, q, :] = m3[part * half1:(part + 1) * half1]
    return big.reshape(half1 * batch, pairs * 2 * FFT_N1)


@functools.lru_cache(maxsize=None)
def _dft_consts_direct(length):
    n = 2 * length
    f = np.arange(n)[:, None]
    t = np.arange(length)[None, :]
    ang = 2.0 * np.pi * f * t / n
    c, s = np.cos(ang), np.sin(ang)
    fwd = np.block([[c, s], [-s, c]])
    ct, st = c.T / n, s.T / n
    inv = np.block([[ct, -st], [st, ct]])
    f32 = lambda a: np.asarray(a, np.float32)
    return f32(fwd), f32(inv), f32(c), f32(-s)


def _hyena_deltas():
    min_decay = math.log(HYENA_DECAY_TARGET) / HYENA_DECAY_SLOW
    max_decay = math.log(HYENA_DECAY_TARGET) / HYENA_DECAY_FAST
    return jnp.abs(jnp.linspace(min_decay, max_decay, D_HYENA, dtype=F32))


def _pos_feats(length, grid):
    d = jnp.arange(length, dtype=jnp.int32)
    t = d.astype(F32) / length
    bands = jnp.arange(1, N_BANDS + 1, dtype=F32)
    ang = 2.0 * math.pi * t[:, None] * bands[None, :]
    feats = [t[:, None], jnp.sin(ang), jnp.cos(ang)]
    if grid:
        rows = max(length // GRID_W, 1)
        col_ang = 2.0 * math.pi * (d % GRID_W).astype(F32) / GRID_W
        row = (d // GRID_W).astype(F32) / rows
        feats += [jnp.sin(col_ang)[:, None], jnp.cos(col_ang)[:, None], row[:, None]]
    else:
        feats.append(jnp.zeros((length, 3), F32))
    z = jnp.concatenate(feats, axis=-1)
    return jnp.pad(z, ((0, 0), (0, LANES - POS_DIM)))


def _ada_kernel(cv_ref, w_ref, b_ref, o_ref):
    s = _silu(cv_ref[...])
    o_ref[...] = _dot(s, w_ref[...]) + b_ref[...]


def _ada_mods(cv, ada_w, ada_b):
    tn = D_MODEL
    return pl.pallas_call(
        _ada_kernel,
        out_shape=jax.ShapeDtypeStruct((DEPTH, 16, 6 * D_MODEL), F32),
        grid=(DEPTH, 6 * D_MODEL // tn),
        in_specs=[
            pl.BlockSpec((16, D_MODEL), lambda l, j: (0, 0)),
            pl.BlockSpec((None, D_MODEL, tn), lambda l, j: (l, 0, j)),
            pl.BlockSpec((None, 1, tn), lambda l, j: (l, 0, j)),
        ],
        out_specs=pl.BlockSpec((None, 16, tn), lambda l, j: (l, 0, j)),
        compiler_params=_cparams(("arbitrary", "arbitrary")),
        name="ada_mods",
    )(cv, ada_w, ada_b.reshape(DEPTH, 1, 6 * D_MODEL))


def _in_kernel(x_ref, prev_ref, next_ref, g_ref, mod_ref, w_ref, sw_ref, sb_ref, cw_ref, cb_ref,
               u_ref, x2_ref, xr_ref, pg_ref, *, batch, tm):
    i = pl.program_id(0)
    last = pl.num_programs(0) - 1
    lead = 2 * batch
    xe = jnp.concatenate([prev_ref[...], x_ref[...], next_ref[...]], axis=0)
    h = _modulate(_rms(xe, g_ref[...]), mod_ref[:, 0:D_MODEL], mod_ref[:, D_MODEL:2 * D_MODEL])
    h = jnp.concatenate([h[:lead] * (i > 0).astype(F32), h[lead:lead + tm],
                         h[lead + tm:] * (i < last).astype(F32)], axis=0)
    p = jnp.dot(h.astype(BF16), w_ref[...], preferred_element_type=F32)

    def tap(d, c0, c1):
        return p[lead + d * batch:lead + d * batch + tm, c0:c1]

    def hyena_branch(j):
        c0, c1 = j * D_HYENA, (j + 1) * D_HYENA
        acc = sb_ref[:, c0:c1] + tap(-1, c0, c1) * sw_ref[0:1, c0:c1]
        acc = acc + tap(0, c0, c1) * sw_ref[1:2, c0:c1]
        return acc + tap(1, c0, c1) * sw_ref[2:3, c0:c1]

    u_ref[...] = hyena_branch(0) * hyena_branch(1)
    x2_ref[...] = hyena_branch(2)
    c0, c1 = 3 * D_HYENA, 3 * D_HYENA + D_LRU
    acc = cb_ref[...] + tap(-2, c0, c1) * cw_ref[0:1, :]
    acc = acc + tap(-1, c0, c1) * cw_ref[1:2, :]
    acc = acc + tap(0, c0, c1) * cw_ref[2:3, :]
    xr_ref[...] = acc + tap(1, c0, c1) * cw_ref[3:4, :]
    pg_ref[...] = tap(0, c1, c1 + D_LRU)


def _in_proj_convs(x, norm_mix, mods, w_in_bf, short_w, short_b, conv_w, conv_b, l, seg, batch, tm=512):
    rows = x.shape[0]
    per_prev = tm // (2 * batch)
    per_next = tm // batch
    n_next = rows // batch
    out = jax.ShapeDtypeStruct((rows, D_HYENA), F32)
    return pl.pallas_call(
        functools.partial(_in_kernel, batch=batch, tm=tm),
        out_shape=(out, out, out, out),
        grid=(rows // tm,),
        in_specs=[
            pl.BlockSpec((tm, D_MODEL), lambda i: (i, 0)),
            pl.BlockSpec((2 * batch, D_MODEL), lambda i: (jnp.maximum(i * per_prev - 1, 0), 0)),
            pl.BlockSpec((batch, D_MODEL), lambda i: (jnp.minimum((i + 1) * per_next, n_next - 1), 0)),
            pl.BlockSpec((None, 1, D_MODEL), lambda i: (l, 0, 0)),
            pl.BlockSpec((None, SUBLANES, 2 * D_MODEL), lambda i: (l, seg, 0)),
            pl.BlockSpec((None, D_MODEL, N_IN), lambda i: (l, 0, 0)),
            pl.BlockSpec((None, 3, 3 * D_HYENA), lambda i: (l, 0, 0)),
            pl.BlockSpec((None, 1, 3 * D_HYENA), lambda i: (l, 0, 0)),
            pl.BlockSpec((None, 4, D_LRU), lambda i: (l, 0, 0)),
            pl.BlockSpec((None, 1, D_LRU), lambda i: (l, 0, 0)),
        ],
        out_specs=(pl.BlockSpec((tm, D_HYENA), lambda i: (i, 0)),) * 4,
        compiler_params=_cparams(("arbitrary",)),
        name="in_proj_convs",
    )(x, x, x, norm_mix.reshape(DEPTH, 1, D_MODEL), mods, w_in_bf,
      short_w, short_b.reshape(DEPTH, 1, 3 * D_HYENA), conv_w, conv_b.reshape(DEPTH, 1, D_LRU))


def _lru_gates(xr, w_ref, ba, bx, sp):
    xb = xr.astype(BF16)
    half = D_LRU // 2
    ra, rx = [], []
    for hf in range(2):
        xs = xb[:, hf * half:(hf + 1) * half]
        ra.append(jnp.dot(xs, w_ref[0, hf], preferred_element_type=F32))
        rx.append(jnp.dot(xs, w_ref[1, hf], preferred_element_type=F32))
    ta = jnp.tanh(jnp.concatenate(ra, axis=1) + ba)
    tx = jnp.tanh(jnp.concatenate(rx, axis=1) + bx)
    c1 = (0.5 * LRU_C) * sp
    neg_log_a = c1 + c1 * ta
    a = jnp.exp2(neg_log_a * (-1.0 / math.log(2.0)))
    mult = jnp.sqrt(jnp.maximum(jnp.tanh(neg_log_a) * (1.0 + a * a), 0.0))
    return a, (0.5 * xr) * (1.0 + tx) * mult


def _softplus(x):
    return jnp.maximum(x, 0.0) + jnp.log1p(jnp.exp(-jnp.abs(x)))


def _lru_kernel(*refs, batch, tm, chunk, reverse):
    if reverse:
        (xr_ref, w_ref, ba_ref, bx_ref, lam_ref, h0_ref, hf_ref, pg_ref,
         y_ref, hl_ref, a_scr, b_scr, h_scr) = refs
    else:
        (xr_ref, w_ref, ba_ref, bx_ref, lam_ref, h0_ref,
         y_ref, hl_ref, a_scr, b_scr, h_scr) = refs
    i = pl.program_id(0)

    @pl.when(i == 0)
    def _():
        h_scr[...] = h0_ref[...].astype(F32)

    sp = _softplus(-lam_ref[...])
    ba = 0.5 * ba_ref[...]
    bx = 0.5 * bx_ref[...]

    def fill(c, carry):
        r = pl.multiple_of(c * chunk, chunk)
        a, b = _lru_gates(xr_ref[pl.ds(r, chunk), :], w_ref, ba, bx, sp)
        a_scr[pl.ds(r, chunk), :] = a
        b_scr[pl.ds(r, chunk), :] = b
        return carry

    lax.fori_loop(0, tm // chunk, fill, 0)

    steps = tm // batch

    def step(s, h):
        t = (steps - 1 - s) if reverse else s
        r = pl.multiple_of(t * batch, batch)
        h = a_scr[pl.ds(r, batch), :] * h + b_scr[pl.ds(r, batch), :]
        y_ref[pl.ds(r, batch), :] = h
        return h

    h = lax.fori_loop(0, steps, step, h_scr[...], unroll=8)
    h_scr[...] = h
    if reverse:
        y_ref[...] = (hf_ref[...] + y_ref[...]) * _gelu_tanh(pg_ref[...])

    @pl.when(i == pl.num_programs(0) - 1)
    def _():
        hl_ref[...] = h


def _lru_scan(xr, wbd, lru_ba, lru_bx, lru_lambda, h0, h0_block, l, direction, batch,
              hf=None, pg=None, tm=2048, chunk=256):
    rows = xr.shape[0]
    nt = rows // tm
    reverse = direction == 1
    row_map = (lambda i: (nt - 1 - i, 0)) if reverse else (lambda i: (i, 0))
    vec = lambda: pl.BlockSpec((None, None, 1, D_LRU), lambda i: (l, direction, 0, 0))
    in_specs = [
        pl.BlockSpec((tm, D_LRU), row_map),
        pl.BlockSpec((None, None, 2, 2, D_LRU // 2, D_LRU // 2), lambda i: (l, direction, 0, 0, 0, 0)),
        vec(), vec(), vec(),
        pl.BlockSpec((batch, D_LRU), lambda i: h0_block),
    ]
    args = [xr, wbd, lru_ba.reshape(DEPTH, 2, 1, D_LRU), lru_bx.reshape(DEPTH, 2, 1, D_LRU),
            lru_lambda.reshape(DEPTH, 2, 1, D_LRU), h0]
    if reverse:
        in_specs += [pl.BlockSpec((tm, D_LRU), row_map), pl.BlockSpec((tm, D_LRU), row_map)]
        args += [hf, pg]
    return pl.pallas_call(
        functools.partial(_lru_kernel, batch=batch, tm=tm, chunk=chunk, reverse=reverse),
        out_shape=(jax.ShapeDtypeStruct((rows, D_LRU), F32), jax.ShapeDtypeStruct((batch, D_LRU), F32)),
        grid=(nt,),
        in_specs=in_specs,
        out_specs=(pl.BlockSpec((tm, D_LRU), row_map), pl.BlockSpec((batch, D_LRU), lambda i: (0, 0))),
        scratch_shapes=[pltpu.VMEM((tm, D_LRU), F32), pltpu.VMEM((tm, D_LRU), F32),
                        pltpu.VMEM((batch, D_LRU), F32)],
        compiler_params=_cparams(("arbitrary",)),
        name="lru_rev" if reverse else "lru_fwd",
    )(*args)


def _filter_kernel(z_ref, w1_ref, b1_ref, w2_ref, b2_ref, w3_ref, b3_ref, fr_ref, dl_ref,
                   ks_ref, kd_ref, sum_ref):
    i = pl.program_id(0)
    hi = lax.Precision.HIGHEST
    z = z_ref[...]
    h1 = jnp.sin(fr_ref[0:1, :] * (jnp.dot(z, w1_ref[...], precision=hi, preferred_element_type=F32)
                                   + b1_ref[...]))
    h2 = jnp.sin(fr_ref[1:2, :] * (jnp.dot(h1, w2_ref[...], precision=hi, preferred_element_type=F32)
                                   + b2_ref[...]))
    k = jnp.dot(h2, w3_ref[...], precision=hi, preferred_element_type=F32) + b3_ref[...]
    window = jnp.exp(-z[:, 0:1] * dl_ref[...]) + HYENA_SHIFT
    kf = k[:, :D_HYENA] * window
    kb = k[:, D_HYENA:] * window
    row = lax.broadcasted_iota(jnp.int32, kb.shape, 0) + i * kb.shape[0]
    kb = jnp.where(row == 0, 0.0, kb)
    ks_ref[...] = kf + kb
    kd_ref[...] = kf - kb
    part = jnp.sum(jnp.abs(kf) + jnp.abs(kb), axis=0, keepdims=True)

    @pl.when(i == 0)
    def _():
        sum_ref[...] = part

    @pl.when(i > 0)
    def _():
        sum_ref[...] += part


def _hyena_filter_taps(length, grid, p, l, tl=256, n2_major=False):
    z = _pos_feats(length, grid)
    if n2_major:
        z = z.reshape(length // FFT_N2, FFT_N2, LANES).transpose(1, 0, 2).reshape(length, LANES)
    w1 = jnp.pad(p['filt_w1'], ((0, 0), (0, LANES - POS_DIM), (0, 0)))
    tap = jax.ShapeDtypeStruct((length, D_HYENA), F32)
    h = FILTER_HIDDEN
    return pl.pallas_call(
        _filter_kernel,
        out_shape=(tap, tap, jax.ShapeDtypeStruct((1, D_HYENA), F32)),
        grid=(length // tl,),
        in_specs=[
            pl.BlockSpec((tl, LANES), lambda i: (i, 0)),
            pl.BlockSpec((None, LANES, h), lambda i: (l, 0, 0)),
            pl.BlockSpec((None, 1, h), lambda i: (l, 0, 0)),
            pl.BlockSpec((None, h, h), lambda i: (l, 0, 0)),
            pl.BlockSpec((None, 1, h), lambda i: (l, 0, 0)),
            pl.BlockSpec((None, h, 2 * D_HYENA), lambda i: (l, 0, 0)),
            pl.BlockSpec((None, 1, 2 * D_HYENA), lambda i: (l, 0, 0)),
            pl.BlockSpec((None, 2, h), lambda i: (l, 0, 0)),
            pl.BlockSpec((1, D_HYENA), lambda i: (0, 0)),
        ],
        out_specs=(pl.BlockSpec((tl, D_HYENA), lambda i: (i, 0)),
                   pl.BlockSpec((tl, D_HYENA), lambda i: (i, 0)),
                   pl.BlockSpec((1, D_HYENA), lambda i: (0, 0))),
        compiler_params=_cparams(("arbitrary",)),
        name="hyena_filter",
    )(z, w1, p['filt_b1'].reshape(DEPTH, 1, h), p['filt_w2'], p['filt_b2'].reshape(DEPTH, 1, h),
      p['filt_w3'], p['filt_b3'].reshape(DEPTH, 1, 2 * D_HYENA), p['filt_freq'],
      _hyena_deltas().reshape(1, D_HYENA))


def _spec_direct_kernel(c_ref, s_ref, ks_ref, kd_ref, sum_ref, o_ref):
    scale = 1.0 / (sum_ref[...] + EPS)
    o_ref[0] = _dot(c_ref[...], ks_ref[...], x3=True) * scale
    o_ref[1] = _dot(s_ref[...], kd_ref[...], x3=True) * scale


def _filter_spectrum_direct(ks, kd, ksum):
    length = ks.shape[0]
    n = 2 * length
    _, _, c, ms = _dft_consts_direct(length)
    return pl.pallas_call(
        _spec_direct_kernel,
        out_shape=jax.ShapeDtypeStruct((2, n, D_HYENA), F32),
        compiler_params=_cparams(()),
        name="filter_spectrum_direct",
    )(jnp.asarray(c), jnp.asarray(ms), ks, kd, ksum)


def _conv_direct_kernel(u_ref, x2_ref, kh_ref, fwd_ref, inv_ref, bias_ref, o_ref, *, length, batch):
    n = 2 * length
    kr, ki = kh_ref[0], kh_ref[1]
    for q in range(batch // 2):
        z = jnp.concatenate([u_ref[:, 2 * q, :], u_ref[:, 2 * q + 1, :]], axis=0)
        x = _dot(fwd_ref[...], z, x3=True)
        xr, xi = x[:n], x[n:]
        y = jnp.concatenate([xr * kr - xi * ki, xr * ki + xi * kr], axis=0)
        conv = _dot(inv_ref[...], y, x3=True)
        o_ref[:, 2 * q, :] = conv[:length]
        o_ref[:, 2 * q + 1, :] = conv[length:]
    o_ref[...] = x2_ref[...] * (o_ref[...] + u_ref[...] * bias_ref[...])


def _hyena_conv_direct(u, x2, khat, hyena_bias, l, batch, cw=256):
    rows = u.shape[0]
    length = rows // batch
    fwd, inv, _, _ = _dft_consts_direct(length)
    blk = pl.BlockSpec((length, batch, cw), lambda j: (0, 0, j))
    out = pl.pallas_call(
        functools.partial(_conv_direct_kernel, length=length, batch=batch),
        out_shape=jax.ShapeDtypeStruct((length, batch, D_HYENA), F32),
        grid=(D_HYENA // cw,),
        in_specs=[blk, blk,
                  pl.BlockSpec((2, 2 * length, cw), lambda j: (0, 0, j)),
                  pl.BlockSpec(fwd.shape, lambda j: (0, 0)),
                  pl.BlockSpec(inv.shape, lambda j: (0, 0)),
                  pl.BlockSpec((None, 1, cw), lambda j: (l, 0, j))],
        out_specs=blk,
        compiler_params=_cparams(("arbitrary",)),
        name="hyena_conv_direct",
    )(u.reshape(length, batch, D_HYENA), x2.reshape(length, batch, D_HYENA), khat,
      jnp.asarray(fwd), jnp.asarray(inv), hyena_bias.reshape(DEPTH, 1, D_HYENA))
    return out.reshape(rows, D_HYENA)


def _stage1_kernel(x_ref, m_ref, o_ref, *, batch):
    m = m_ref[...].astype(BF16)
    for j in range(N2_BLOCK):
        for q in range(batch // 2):
            z = jnp.concatenate([x_ref[:, j, 2 * q, :], x_ref[:, j, 2 * q + 1, :]], axis=0)
            o_ref[j, q] = jnp.dot(m, z.astype(BF16), preferred_element_type=F32)


def _fft_stage1(u, batch, cw=256):
    c = u.shape[1]
    half1 = FFT_N1 // 2
    pairs = batch // 2
    m1, _, _, _ = _dft_consts()
    return pl.pallas_call(
        functools.partial(_stage1_kernel, batch=batch),
        out_shape=jax.ShapeDtypeStruct((FFT_N2, pairs, 2 * FFT_N1, c), F32),
        grid=(FFT_N2 // N2_BLOCK, c // cw),
        in_specs=[pl.BlockSpec((half1, N2_BLOCK, batch, cw), lambda j, k: (0, j, 0, k)),
                  pl.BlockSpec(m1.shape, lambda j, k: (0, 0))],
        out_specs=pl.BlockSpec((N2_BLOCK, pairs, 2 * FFT_N1, cw), lambda j, k: (j, 0, 0, k)),
        compiler_params=_cparams(("arbitrary", "arbitrary")),
        name="fft_stage1",
    )(u.reshape(half1, FFT_N2, batch, c), jnp.asarray(m1))


def _stage1_filter_kernel(ks_ref, kd_ref, m_ref, o_ref):
    m = m_ref[:, :FFT_N1 // 2]
    for j in range(N2_BLOCK):
        for g, ref in enumerate((ks_ref, kd_ref)):
            o_ref[j, g] = _dot(m, ref[j], x3=True)


def _stage2_filter_kernel(a_ref, g_ref, sum_ref, o_ref):
    scale = 1.0 / (sum_ref[...] + EPS)
    res = []
    for j in range(K1_BLOCK):
        bs = jnp.concatenate([a_ref[:, 0, 0, j, :], a_ref[:, 0, 1, j, :]], axis=0)
        bd = jnp.concatenate([a_ref[:, 1, 0, j, :], a_ref[:, 1, 1, j, :]], axis=0)
        res.append((_dot(g_ref[j], bs, x3=True), _dot(g_ref[j], bd, x3=True)))
    for j in range(K1_BLOCK):
        o_ref[0, j] = res[j][0][:FFT_N2] * scale
        o_ref[1, j] = res[j][1][FFT_N2:] * scale


def _filter_spectrum_2stage(ks, kd, ksum):
    c = D_HYENA
    half1 = FFT_N1 // 2
    m1, g2, _, _ = _dft_consts()
    tap = pl.BlockSpec((N2_BLOCK, half1, c), lambda j: (j, 0, 0))
    a = pl.pallas_call(
        _stage1_filter_kernel,
        out_shape=jax.ShapeDtypeStruct((FFT_N2, 2, 2 * FFT_N1, c), F32),
        grid=(FFT_N2 // N2_BLOCK,),
        in_specs=[tap, tap, pl.BlockSpec(m1.shape, lambda j: (0, 0))],
        out_specs=pl.BlockSpec((N2_BLOCK, 2, 2 * FFT_N1, c), lambda j: (j, 0, 0, 0)),
        compiler_params=_cparams(("arbitrary",)),
        name="filter_spectrum_stage1",
    )(ks.reshape(FFT_N2, half1, c), kd.reshape(FFT_N2, half1, c), jnp.asarray(m1))
    return pl.pallas_call(
        _stage2_filter_kernel,
        out_shape=jax.ShapeDtypeStruct((2, FFT_N1, FFT_N2, c), F32),
        grid=(FFT_N1 // K1_BLOCK,),
        in_specs=[pl.BlockSpec((FFT_N2, 2, 2, K1_BLOCK, c), lambda i: (0, 0, 0, i, 0)),
                  pl.BlockSpec((K1_BLOCK, 2 * FFT_N2, 2 * FFT_N2), lambda i: (i, 0, 0)),
                  pl.BlockSpec((1, c), lambda i: (0, 0))],
        out_specs=pl.BlockSpec((2, K1_BLOCK, FFT_N2, c), lambda i: (0, i, 0, 0)),
        compiler_params=_cparams(("arbitrary",)),
        name="filter_spectrum_stage2",
    )(a.reshape(FFT_N2, 2, 2, FFT_N1, c), jnp.asarray(g2), ksum)


def _stage2_kernel(a_ref, kh_ref, g_ref, gt_ref, o_ref):
    xs = []
    for j in range(K1_BLOCK):
        b = jnp.concatenate([a_ref[:, 0, j, :], a_ref[:, 1, j, :]], axis=0)
        xs.append(_dot(g_ref[j], b))
    ys = []
    for j in range(K1_BLOCK):
        xr, xi = xs[j][:FFT_N2], xs[j][FFT_N2:]
        kr, ki = kh_ref[0, j], kh_ref[1, j]
        ys.append(jnp.concatenate([xr * kr - xi * ki, xr * ki + xi * kr], axis=0).astype(BF16))
    for j in range(K1_BLOCK):
        o_ref[j] = jnp.dot(gt_ref[j].astype(BF16), ys[j], preferred_element_type=F32)


def _fft_stage2(a, khat):
    _, g2, g2t, _ = _dft_consts()
    c = D_HYENA
    pairs = a.shape[1]
    gblk = pl.BlockSpec((K1_BLOCK, 2 * FFT_N2, 2 * FFT_N2), lambda i, q: (i, 0, 0))
    return pl.pallas_call(
        _stage2_kernel,
        out_shape=jax.ShapeDtypeStruct((FFT_N1, pairs, 2 * FFT_N2, c), F32),
        grid=(FFT_N1 // K1_BLOCK, pairs),
        in_specs=[pl.BlockSpec((FFT_N2, None, 2, K1_BLOCK, c), lambda i, q: (0, q, 0, i, 0)),
                  pl.BlockSpec((2, K1_BLOCK, FFT_N2, c), lambda i, q: (0, i, 0, 0)), gblk, gblk],
        out_specs=pl.BlockSpec((K1_BLOCK, None, 2 * FFT_N2, c), lambda i, q: (i, q, 0, 0)),
        compiler_params=_cparams(("arbitrary", "arbitrary")),
        name="fft_stage2",
    )(a.reshape(FFT_N2, pairs, 2, FFT_N1, c), khat, jnp.asarray(g2), jnp.asarray(g2t))


def _stage3_kernel(a_ref, m_ref, u_ref, x2_ref, bias_ref, o_ref, *, batch):
    half1 = FFT_N1 // 2
    m = m_ref[...].astype(BF16)
    for j in range(N2_BLOCK):
        b = jnp.concatenate([a_ref[:, q, ri, j, :] for q in range(batch // 2) for ri in range(2)], axis=0)
        y = jnp.dot(m, b.astype(BF16), preferred_element_type=F32)
        o_ref[:, j] = y.reshape(half1, batch, y.shape[-1])
    o_ref[...] = x2_ref[...] * (o_ref[...] + u_ref[...] * bias_ref[...])


def _fft_stage3(a, u, x2, hyena_bias, l, batch, cw=256):
    rows, c = u.shape
    half1 = FFT_N1 // 2
    pairs = batch // 2
    m3 = _inverse_stage1_interleaved(batch)
    blk = pl.BlockSpec((half1, N2_BLOCK, batch, cw), lambda j, k: (0, j, 0, k))
    out = pl.pallas_call(
        functools.partial(_stage3_kernel, batch=batch),
        out_shape=jax.ShapeDtypeStruct((half1, FFT_N2, batch, c), F32),
        grid=(FFT_N2 // N2_BLOCK, c // cw),
        in_specs=[pl.BlockSpec((FFT_N1, pairs, 2, N2_BLOCK, cw), lambda j, k: (0, 0, 0, j, k)),
                  pl.BlockSpec(m3.shape, lambda j, k: (0, 0)),
                  blk, blk,
                  pl.BlockSpec((None, 1, cw), lambda j, k: (l, 0, k))],
        out_specs=blk,
        compiler_params=_cparams(("arbitrary", "arbitrary")),
        name="fft_stage3",
    )(a.reshape(FFT_N1, pairs, 2, FFT_N2, c), jnp.asarray(m3), u.reshape(half1, FFT_N2, batch, c),
      x2.reshape(half1, FFT_N2, batch, c), hyena_bias.reshape(DEPTH, 1, c))
    return out.reshape(rows, c)


def _hyena_conv_2stage(u, x2, khat, hyena_bias, l, batch):
    a = _fft_stage1(u, batch)
    a = _fft_stage2(a, khat)
    return _fft_stage3(a, u, x2, hyena_bias, l, batch)


def _mix_and_prenorm(hy_ref, lr_ref, x_ref, w_ref, g_ref, mod_ref):
    d = D_MODEL
    mixed = (jnp.dot(hy_ref[...].astype(BF16), w_ref[0:D_HYENA, :], preferred_element_type=F32)
             + jnp.dot(lr_ref[...].astype(BF16), w_ref[D_HYENA:, :], preferred_element_type=F32))
    rows = mixed.shape[0]
    g1 = mod_ref[:, 2 * d:3 * d]
    x = x_ref[...] + (mixed.reshape(rows // SUBLANES, SUBLANES, d) * g1[None]).reshape(rows, d)
    h = _modulate(_rms(x, g_ref[...]), mod_ref[:, 3 * d:4 * d], mod_ref[:, 4 * d:5 * d])
    return x, h.astype(BF16)


def _out_kernel(hy_ref, lr_ref, x_ref, w_ref, g_ref, mod_ref, xo_ref, h_ref):
    xo_ref[...], h_ref[...] = _mix_and_prenorm(hy_ref, lr_ref, x_ref, w_ref, g_ref, mod_ref)


def _out_proj(hy, lr, x, w_out_bf, norm_ffn, mods, l, seg, tm=512):
    rows = x.shape[0]
    d = D_MODEL
    return pl.pallas_call(
        _out_kernel,
        out_shape=(jax.ShapeDtypeStruct((rows, d), F32), jax.ShapeDtypeStruct((rows, d), BF16)),
        grid=(rows // tm,),
        in_specs=[
            pl.BlockSpec((tm, D_HYENA), lambda i: (i, 0)),
            pl.BlockSpec((tm, D_LRU), lambda i: (i, 0)),
            pl.BlockSpec((tm, d), lambda i: (i, 0)),
            pl.BlockSpec((None, D_HYENA + D_LRU, d), lambda i: (l, 0, 0)),
            pl.BlockSpec((None, 1, d), lambda i: (l, 0, 0)),
            pl.BlockSpec((None, SUBLANES, 6 * d), lambda i: (l, seg, 0)),
        ],
        out_specs=(pl.BlockSpec((tm, d), lambda i: (i, 0)), pl.BlockSpec((tm, d), lambda i: (i, 0))),
        compiler_params=_cparams(("arbitrary",)),
        name="out_proj",
    )(hy, lr, x, w_out_bf, norm_ffn.reshape(DEPTH, 1, d), mods)


def _gate_residual(x, acc, g2):
    rows, d = acc.shape
    return x + (acc.reshape(rows // SUBLANES, SUBLANES, d) * g2[None]).reshape(rows, d)


def _swiglu_chunk(h, wgu_ref, wd_ref):
    gu = jnp.dot(h, wgu_ref[...], preferred_element_type=F32)
    fc = gu.shape[1] // 2
    return jnp.dot((_silu(gu[:, :fc]) * gu[:, fc:]).astype(BF16), wd_ref[...], preferred_element_type=F32)


def _chunked_gate_up(w_gu, fc):
    parts = []
    for f in range(D_FF // fc):
        parts += [w_gu[..., f * fc:(f + 1) * fc], w_gu[..., D_FF + f * fc:D_FF + (f + 1) * fc]]
    return jnp.concatenate(parts, axis=-1)


def _mix_ffn_kernel(hy_ref, lr_ref, x_ref, w_ref, g_ref, mod_ref, wgu_ref, wd_ref,
                    o_ref, x_scr, h_scr, acc_ref):
    f = pl.program_id(1)

    @pl.when(f == 0)
    def _():
        x_scr[...], h_scr[...] = _mix_and_prenorm(hy_ref, lr_ref, x_ref, w_ref, g_ref, mod_ref)

    part = _swiglu_chunk(h_scr[...], wgu_ref, wd_ref)

    @pl.when(f == 0)
    def _():
        acc_ref[...] = part

    @pl.when(f > 0)
    def _():
        acc_ref[...] += part

    @pl.when(f == pl.num_programs(1) - 1)
    def _():
        o_ref[...] = _gate_residual(x_scr[...], acc_ref[...], mod_ref[:, 5 * D_MODEL:])


def _out_proj_ffn_dense(hy, lr, x, w_out_bf, norm_ffn, w_gu_bf, w_down_bf, mods, m, l, seg, tm=512, fc=FFN_CHUNK):
    rows = x.shape[0]
    d = D_MODEL
    nf = D_FF // fc
    return pl.pallas_call(
        _mix_ffn_kernel,
        out_shape=jax.ShapeDtypeStruct((rows, d), F32),
        grid=(rows // tm, nf),
        in_specs=[
            pl.BlockSpec((tm, D_HYENA), lambda i, f: (i, 0)),
            pl.BlockSpec((tm, D_LRU), lambda i, f: (i, 0)),
            pl.BlockSpec((tm, d), lambda i, f: (i, 0)),
            pl.BlockSpec((None, D_HYENA + D_LRU, d), lambda i, f: (l, 0, 0)),
            pl.BlockSpec((None, 1, d), lambda i, f: (l, 0, 0)),
            pl.BlockSpec((None, SUBLANES, 6 * d), lambda i, f: (l, seg, 0)),
            pl.BlockSpec((None, d, 2 * fc), lambda i, f: (m, 0, f)),
            pl.BlockSpec((None, fc, d), lambda i, f: (m, f, 0)),
        ],
        out_specs=pl.BlockSpec((tm, d), lambda i, f: (i, 0)),
        scratch_shapes=[pltpu.VMEM((tm, d), F32), pltpu.VMEM((tm, d), BF16), pltpu.VMEM((tm, d), F32)],
        compiler_params=_cparams(("arbitrary", "arbitrary")),
        name="out_proj_ffn_dense",
    )(hy, lr, x, w_out_bf, norm_ffn.reshape(DEPTH, 1, d), mods, w_gu_bf, w_down_bf)


def _route_top2(logits):
    lane = lax.broadcasted_iota(jnp.int32, logits.shape, 1)
    neg = jnp.float32(-jnp.inf)
    v = jnp.where(lane < N_EXPERTS, logits, neg)
    m1 = jnp.max(v, axis=-1, keepdims=True)
    i1 = jnp.min(jnp.where(v == m1, lane, LANES), axis=-1, keepdims=True)
    v2 = jnp.where(lane == i1, neg, v)
    m2 = jnp.max(v2, axis=-1, keepdims=True)
    i2 = jnp.min(jnp.where(v2 == m2, lane, LANES), axis=-1, keepdims=True)
    e2 = jnp.exp(m2 - m1)
    w1 = 1.0 / (1.0 + e2)
    w2 = e2 / (1.0 + e2)
    first = lane == i1
    second = lane == i2
    comb = jnp.where(first, w1, 0.0) + jnp.where(second, w2, 0.0)
    return comb, jnp.where(first | second, 1.0, 0.0)


MOE_TILE = 2048
MOE_CH = 256
MOE_SB = 128
MOE_WIN = 256
MOE_CS_ROWS = 16


def _router_kernel(h_ref, r_ref, comb_ref, key_ref, keyt_ref, cs_ref, *, tt):
    nch = tt // MOE_CH
    r_hi, r_lo = _split(r_ref[...])
    h = h_ref[...]
    logits = (jnp.dot(h, r_hi, preferred_element_type=F32) + jnp.dot(h, r_lo, preferred_element_type=F32))
    comb, sel = _route_top2(logits)
    comb_ref[...] = comb
    rr = lax.broadcasted_iota(jnp.int32, (MOE_CH, MOE_CH), 0)
    cc = lax.broadcasted_iota(jnp.int32, (MOE_CH, MOE_CH), 1)
    strict_lower = jnp.where(cc < rr, 1.0, 0.0).astype(BF16)
    cs_ref[...] = jnp.zeros(cs_ref.shape, jnp.int32)
    carry = jnp.zeros((1, LANES), F32)
    for c in range(nch):
        selc = sel[c * MOE_CH:(c + 1) * MOE_CH]
        cs_ref[c:c + 1, :] = carry.astype(jnp.int32)
        rank = jnp.dot(strict_lower, selc.astype(BF16), preferred_element_type=F32) + carry
        key_ref[c * MOE_CH:(c + 1) * MOE_CH, :] = jnp.where(selc > 0.0, rank, -1.0)
        carry = carry + jnp.sum(selc, axis=0, keepdims=True)
    cs_ref[nch:nch + 1, :] = carry.astype(jnp.int32)
    keyt = jnp.transpose(key_ref[...])
    for e in range(N_EXPERTS):
        for c in range(nch):
            keyt_ref[e, c] = keyt[e:e + 1, c * MOE_CH:(c + 1) * MOE_CH]


def _moe_route(h, router_pad, m, tt):
    rows, d = h.shape
    nt = rows // tt
    nch = tt // MOE_CH
    tok = jax.ShapeDtypeStruct((rows, LANES), F32)
    return pl.pallas_call(
        functools.partial(_router_kernel, tt=tt),
        out_shape=(tok, tok,
                   jax.ShapeDtypeStruct((N_EXPERTS, rows // MOE_CH, 1, MOE_CH), F32),
                   jax.ShapeDtypeStruct((nt * MOE_CS_ROWS, LANES), jnp.int32)),
        grid=(nt,),
        in_specs=[pl.BlockSpec((tt, d), lambda i: (i, 0)),
                  pl.BlockSpec((None, d, LANES), lambda i: (m, 0, 0))],
        out_specs=(pl.BlockSpec((tt, LANES), lambda i: (i, 0)),
                   pl.BlockSpec((tt, LANES), lambda i: (i, 0)),
                   pl.BlockSpec((N_EXPERTS, nch, 1, MOE_CH), lambda i: (0, i, 0, 0)),
                   pl.BlockSpec((MOE_CS_ROWS, LANES), lambda i: (i, 0))),
        compiler_params=_cparams(("arbitrary",)),
        name="moe_route",
    )(h, router_pad)


def _moe_kernel(cs_ref, h_ref, x_ref, comb_ref, key_ref, keyt_ref, wgu_ref, wd_ref, mod_ref,
                o_ref, xg_ref, y_ref, *, tt):
    i = pl.program_id(0)
    e = pl.program_id(1)
    f = pl.program_id(2)
    last_f = pl.num_programs(2) - 1
    nch = tt // MOE_CH
    base = i * ((nch + 1) * N_EXPERTS) + e

    def count_before(c):
        return cs_ref[base + c * N_EXPERTS]

    n_sub = (count_before(nch) + (MOE_SB - 1)) // MOE_SB

    def windows(c):
        lo = count_before(c)
        hi = count_before(c + 1)
        w0 = (lo // MOE_SB) * MOE_SB
        return w0, (hi - w0 + (MOE_WIN - 1)) // MOE_WIN

    def row_block(r0, size):
        return pl.ds(pl.multiple_of(r0, MOE_SB), size)

    @pl.when((e == 0) & (f == 0))
    def _():
        o_ref[...] = jnp.zeros_like(o_ref)

    @pl.when(f == 0)
    def _():
        def clear(j, carry):
            xg_ref[row_block(j * MOE_SB, MOE_SB), :] = jnp.zeros((MOE_SB, D_MODEL), BF16)
            return carry

        lax.fori_loop(0, n_sub + MOE_WIN // MOE_SB, clear, 0)
        y_ref[row_block(n_sub * MOE_SB, MOE_WIN), :] = jnp.zeros((MOE_WIN, D_MODEL), BF16)

        def gather(c, r0):
            row = (lax.broadcasted_iota(jnp.int32, (MOE_WIN, MOE_CH), 0) + r0).astype(F32)
            sel = jnp.where(keyt_ref[e, c] == row, 1.0, 0.0).astype(BF16)
            part = jnp.dot(sel, h_ref[c * MOE_CH:(c + 1) * MOE_CH, :], preferred_element_type=F32)
            xg_ref[row_block(r0, MOE_WIN), :] += part.astype(BF16)

        starts = [windows(c) for c in range(nch)]
        for c in range(nch):
            gather(c, starts[c][0])
        for c in range(nch):
            w0, n_win = starts[c]
            lax.fori_loop(1, n_win, lambda k, carry, c=c, w0=w0: (gather(c, w0 + k * MOE_WIN), carry)[1], 0)

    def expert_rows(r0, size):
        rows = row_block(r0, size)
        xg = xg_ref[rows, :]
        y = _swiglu_chunk(xg, wgu_ref, wd_ref)

        @pl.when(f == 0)
        def _():
            y_ref[rows, :] = y.astype(BF16)

        @pl.when(f > 0)
        def _():
            y_ref[rows, :] = (y_ref[rows, :].astype(F32) + y).astype(BF16)

    n_pair = n_sub // 2
    lax.fori_loop(0, n_pair, lambda j, carry: (expert_rows(j * (2 * MOE_SB), 2 * MOE_SB), carry)[1], 0)

    @pl.when(n_sub % 2 == 1)
    def _():
        expert_rows(n_pair * (2 * MOE_SB), MOE_SB)

    @pl.when(f == last_f)
    def _():
        lane_e = lax.broadcasted_iota(jnp.int32, (MOE_CH, LANES), 1) == e

        def scatter(c, r0, key, w):
            pos = (lax.broadcasted_iota(jnp.int32, (MOE_CH, MOE_WIN), 1) + r0).astype(F32)
            sel = jnp.where(key == pos, 1.0, 0.0).astype(BF16)
            part = jnp.dot(sel, y_ref[row_block(r0, MOE_WIN), :], preferred_element_type=F32)
            o_ref[c * MOE_CH:(c + 1) * MOE_CH, :] += w * part

        starts = [windows(c) for c in range(nch)]
        cols = []
        for c in range(nch):
            tok = slice(c * MOE_CH, (c + 1) * MOE_CH)
            key = jnp.sum(jnp.where(lane_e, key_ref[tok, :], 0.0), axis=-1, keepdims=True)
            w = jnp.sum(jnp.where(lane_e, comb_ref[tok, :], 0.0), axis=-1, keepdims=True)
            cols.append((key, w))
            scatter(c, starts[c][0], key, w)
        for c in range(nch):
            w0, n_win = starts[c]
            key, w = cols[c]
            lax.fori_loop(1, n_win,
                          lambda k, carry, c=c, w0=w0, key=key, w=w:
                          (scatter(c, w0 + k * MOE_WIN, key, w), carry)[1], 0)

    @pl.when((e == pl.num_programs(1) - 1) & (f == last_f))
    def _():
        o_ref[...] = _gate_residual(x_ref[...], o_ref[...], mod_ref[...])


def _ffn_moe(h, x, router_pad, w_gu_bf, w_down_bf, mods, m, l, seg, fc=FFN_CHUNK):
    rows = x.shape[0]
    d = D_MODEL
    nf = D_FF // fc
    tt = MOE_TILE
    nt = rows // tt
    nch = tt // MOE_CH
    comb, key, keyt, cs = _moe_route(h, router_pad, m, tt)
    counts = cs.reshape(nt, MOE_CS_ROWS, LANES)[:, :nch + 1, :N_EXPERTS].reshape(-1)
    once = pl.Buffered(1)
    grid_spec = pltpu.PrefetchScalarGridSpec(
        num_scalar_prefetch=1,
        grid=(nt, N_EXPERTS, nf),
        in_specs=[
            pl.BlockSpec((tt, d), lambda i, e, f, cs: (i, 0), pipeline_mode=once),
            pl.BlockSpec((tt, d), lambda i, e, f, cs: (i, 0), pipeline_mode=once),
            pl.BlockSpec((tt, LANES), lambda i, e, f, cs: (i, 0), pipeline_mode=once),
            pl.BlockSpec((tt, LANES), lambda i, e, f, cs: (i, 0), pipeline_mode=once),
            pl.BlockSpec((N_EXPERTS, nch, 1, MOE_CH), lambda i, e, f, cs: (0, i, 0, 0), pipeline_mode=once),
            pl.BlockSpec((None, None, d, 2 * fc), lambda i, e, f, cs: (m, e, 0, f)),
            pl.BlockSpec((None, None, fc, d), lambda i, e, f, cs: (m, e, f, 0)),
            pl.BlockSpec((None, SUBLANES, d), lambda i, e, f, cs: (l, seg, 5)),
        ],
        out_specs=pl.BlockSpec((tt, d), lambda i, e, f, cs: (i, 0), pipeline_mode=once),
        scratch_shapes=[pltpu.VMEM((tt + MOE_SB + MOE_WIN, d), BF16),
                        pltpu.VMEM((tt + MOE_SB + MOE_WIN, d), BF16)],
    )
    return pl.pallas_call(
        functools.partial(_moe_kernel, tt=tt),
        out_shape=jax.ShapeDtypeStruct((rows, d), F32),
        grid_spec=grid_spec,
        compiler_params=_cparams(("arbitrary", "arbitrary", "arbitrary")),
        name="ffn_moe",
    )(counts, h, x, comb, key, keyt, w_gu_bf, w_down_bf, mods)


def _final_kernel(x_ref, g_ref, o_ref, *, batch):
    for b in range(batch):
        o_ref[b] = _rms(x_ref[:, b, :], g_ref[...])


def _final_norm(x, norm_final, batch, tl=64):
    rows, d = x.shape
    length = rows // batch
    return pl.pallas_call(
        functools.partial(_final_kernel, batch=batch),
        out_shape=jax.ShapeDtypeStruct((batch, length, d), F32),
        grid=(length // tl,),
        in_specs=[pl.BlockSpec((tl, batch, d), lambda i: (i, 0, 0)),
                  pl.BlockSpec((1, d), lambda i: (0, 0))],
        out_specs=pl.BlockSpec((batch, tl, d), lambda i: (0, i, 0)),
        compiler_params=_cparams(("arbitrary",)),
        name="final_norm",
    )(x.reshape(length, batch, d), norm_final.reshape(1, d))


def _time_major_kernel(x_ref, o_ref, *, batch):
    for t in range(o_ref.shape[0]):
        o_ref[t] = x_ref[:, t, :]


def _to_time_major(x, tl=64):
    batch, length, d = x.shape
    out = pl.pallas_call(
        functools.partial(_time_major_kernel, batch=batch),
        out_shape=jax.ShapeDtypeStruct((length, batch, d), x.dtype),
        grid=(length // tl,),
        in_specs=[pl.BlockSpec((batch, tl, d), lambda i: (0, i, 0))],
        out_specs=pl.BlockSpec((tl, batch, d), lambda i: (i, 0, 0)),
        compiler_params=_cparams(("arbitrary",)),
        name="to_time_major",
    )(x)
    return out.reshape(length * batch, d)


def _block_diag_heads(w):
    hh = N_LRU_HEADS // 2
    w = w.reshape(DEPTH, 2, 2, hh, LRU_HEAD_DIM, LRU_HEAD_DIM)
    eye = jnp.eye(hh, dtype=w.dtype)
    bd = jnp.einsum('ldgpij,pq->ldgpiqj', w, eye)
    return bd.reshape(DEPTH, 2, 2, hh * LRU_HEAD_DIM, hh * LRU_HEAD_DIM)


def kernel(x_prompt, x_sample, state_lru, c, c_ctx, norm_mix, norm_ffn, norm_final, ada_w, ada_b,
           w_in, w_out, hyena_short_w, hyena_short_b, filt_w1, filt_b1, filt_w2, filt_b2, filt_w3,
           filt_b3, filt_freq, hyena_bias, lru_conv_w, lru_conv_b, lru_wa, lru_ba, lru_wx, lru_bx,
           lru_lambda, ffn_w_gu, ffn_w_down, moe_router, moe_w_gu, moe_w_down):
    bp, lp, d = x_prompt.shape
    bs, ls, _ = x_sample.shape
    assert bs == SUBLANES and bp % SUBLANES == 0 and ls == (FFT_N1 // 2) * FFT_N2

    xs = _to_time_major(x_sample)
    xp = _to_time_major(x_prompt)

    cv = jnp.concatenate([c, jnp.broadcast_to(c_ctx[None, :], (SUBLANES, d))], axis=0)
    mods = _ada_mods(cv, ada_w, ada_b)

    w_in_bf = w_in.astype(BF16)
    w_out_bf = w_out.astype(BF16)
    ffn_gu_bf = _chunked_gate_up(ffn_w_gu, FFN_CHUNK).astype(BF16)
    ffn_down_bf = ffn_w_down.astype(BF16)
    moe_gu_bf = _chunked_gate_up(moe_w_gu, FFN_CHUNK).astype(BF16)
    moe_down_bf = moe_w_down.astype(BF16)
    router_pad = jnp.pad(moe_router, ((0, 0), (0, 0), (0, LANES - N_EXPERTS)))
    wbd = (0.5 * jnp.stack([_block_diag_heads(lru_wa), _block_diag_heads(lru_wx)], axis=2)).astype(BF16)
    filt = dict(filt_w1=filt_w1, filt_b1=filt_b1, filt_w2=filt_w2, filt_b2=filt_b2,
                filt_w3=filt_w3, filt_b3=filt_b3, filt_freq=filt_freq)
    state_flat = state_lru.reshape(bs, DEPTH * 2 * D_LRU)
    zero_state = jnp.zeros((bp, D_LRU), F32)

    new_states = []
    for l in range(DEPTH):
        m = l // 2
        for path in range(2):
            if path == 0:
                x, batch, seg, grid_pos = xp, bp, 1, False
            else:
                x, batch, seg, grid_pos = xs, bs, 0, True
            length = x.shape[0] // batch

            u, x2, xr, pg = _in_proj_convs(x, norm_mix, mods, w_in_bf, hyena_short_w, hyena_short_b,
                                           lru_conv_w, lru_conv_b, l, seg, batch)

            ks, kd, ksum = _hyena_filter_taps(length, grid_pos, filt, l, n2_major=(path == 1))
            if path == 0:
                khat = _filter_spectrum_direct(ks, kd, ksum)
                hy = _hyena_conv_direct(u, x2, khat, hyena_bias, l, batch)
            else:
                khat = _filter_spectrum_2stage(ks, kd, ksum)
                hy = _hyena_conv_2stage(u, x2, khat, hyena_bias, l, batch)

            lru_tm = min(2048, x.shape[0])
            if path == 0:
                h0f, h0f_blk, h0r, h0r_blk = zero_state, (0, 0), zero_state, (0, 0)
            else:
                h0f, h0f_blk, h0r, h0r_blk = state_flat, (0, 2 * l), state_flat, (0, 2 * l + 1)
            hf, hlast_f = _lru_scan(xr, wbd, lru_ba, lru_bx, lru_lambda, h0f, h0f_blk, l, 0, batch,
                                    tm=lru_tm)
            lr, hlast_r = _lru_scan(xr, wbd, lru_ba, lru_bx, lru_lambda, h0r, h0r_blk, l, 1, batch,
                                    hf=hf, pg=pg, tm=lru_tm)
            if path == 0:
                new_states.append(jnp.stack([hlast_f, hlast_r], axis=1))

            if l % 2 == 0:
                x = _out_proj_ffn_dense(hy, lr, x, w_out_bf, norm_ffn, ffn_gu_bf, ffn_down_bf, mods, m, l, seg)
            else:
                x, h = _out_proj(hy, lr, x, w_out_bf, norm_ffn, mods, l, seg)
                x = _ffn_moe(h, x, router_pad, moe_gu_bf, moe_down_bf, mods, m, l, seg)
            if path == 0:
                xp = x
            else:
                xs = x

    new_state_lru = jnp.stack(new_states, axis=1).astype(state_lru.dtype)
    y_prompt = _final_norm(xp, norm_final, bp)
    y_sample = _final_norm(xs, norm_final, bs)
    return (y_prompt, y_sample, new_state_lru)
```

```python
import functools
import math

import numpy as np
import jax
import jax.numpy as jnp
from jax import lax
from jax.experimental import pallas as pl
from jax.experimental.pallas import tpu as pltpu

F32 = jnp.float32
BF16 = jnp.bfloat16

D_MODEL = 1024
DEPTH = 4
D_HYENA = 512
D_LRU = 512
N_IN = 3 * D_HYENA + 2 * D_LRU
N_LRU_HEADS = 8
LRU_HEAD_DIM = D_LRU // N_LRU_HEADS
LRU_C = 8.0
FILTER_HIDDEN = 64
N_BANDS = 4
POS_DIM = 1 + 2 * N_BANDS + 3
GRID_W = 64
HYENA_DECAY_FAST = 0.3
HYENA_DECAY_SLOW = 1.5
HYENA_DECAY_TARGET = 1e-2
HYENA_SHIFT = 0.05
D_FF = 2816
N_EXPERTS = 8
EPS = 1e-6

SUBLANES = 8
LANES = 128
VMEM_LIMIT = 56 * 1024 * 1024

FFT_N1 = 128
FFT_N2 = 64
K1_BLOCK = 16
N2_BLOCK = 8
FFN_CHUNK = D_FF // 2


def _cparams(sem):
    return pltpu.CompilerParams(dimension_semantics=sem, vmem_limit_bytes=VMEM_LIMIT)


def _split(a):
    hi = a.astype(BF16)
    lo = (a - hi.astype(F32)).astype(BF16)
    return hi, lo


def _dot(a, b, x3=False):
    if not x3:
        return jnp.dot(a.astype(BF16), b.astype(BF16), preferred_element_type=F32)
    ah, al = _split(a)
    bh, bl = _split(b)
    return (jnp.dot(ah, bh, preferred_element_type=F32)
            + jnp.dot(ah, bl, preferred_element_type=F32)
            + jnp.dot(al, bh, preferred_element_type=F32))


def _silu(x):
    return x * jax.nn.sigmoid(x)


def _gelu_tanh(x):
    return 0.5 * x * (1.0 + jnp.tanh(math.sqrt(2.0 / math.pi) * (x + 0.044715 * (x * x * x))))


def _rms(x, g):
    return x * lax.rsqrt(jnp.mean(x * x, axis=-1, keepdims=True) + EPS) * g


def _modulate(y, shift, scale):
    rows, d = y.shape
    y3 = y.reshape(rows // SUBLANES, SUBLANES, d)
    return (y3 * (1.0 + scale)[None] + shift[None]).reshape(rows, d)


@functools.lru_cache(maxsize=None)
def _dft_consts():
    n = FFT_N1 * FFT_N2
    half1 = FFT_N1 // 2
    k1 = np.arange(FFT_N1)[:, None]
    n1 = np.arange(half1)[None, :]
    ang = 2.0 * np.pi * k1 * n1 / FFT_N1
    c, s = np.cos(ang), np.sin(ang)
    m1 = np.block([[c, s], [-s, c]])
    k2 = np.arange(FFT_N2)[None, :, None]
    n2 = np.arange(FFT_N2)[None, None, :]
    kk1 = np.arange(FFT_N1)[:, None, None]
    th = 2.0 * np.pi * (n2 * k2 / FFT_N2 + n2 * kk1 / n)
    fr, fi = np.cos(th), -np.sin(th)
    g2 = np.concatenate([np.concatenate([fr, -fi], axis=2), np.concatenate([fi, fr], axis=2)], axis=1)
    g2t = np.transpose(g2, (0, 2, 1))
    cm, sm = c.T / n, s.T / n
    m3 = np.block([[cm, -sm], [sm, cm]])
    f32 = lambda a: np.asarray(a, np.float32)
    return f32(m1), f32(g2), f32(g2t), f32(m3)


@functools.lru_cache(maxsize=None)
def _inverse_stage1_interleaved(batch):
    _, _, _, m3 = _dft_consts()
    half1 = FFT_N1 // 2
    pairs = batch // 2
    big = np.zeros((half1, batch, pairs, 2 * FFT_N1), np.float32)
    for q in range(pairs):
        for part in range(2):
            big[:, 2 * q + part, q, :] = m3[part * half1:(part + 1) * half1]
    return big.reshape(half1 * batch, pairs * 2 * FFT_N1)


@functools.lru_cache(maxsize=None)
def _dft_consts_direct(length):
    n = 2 * length
    f = np.arange(n)[:, None]
    t = np.arange(length)[None, :]
    ang = 2.0 * np.pi * f * t / n
    c, s = np.cos(ang), np.sin(ang)
    fwd = np.block([[c, s], [-s, c]])
    ct, st = c.T / n, s.T / n
    inv = np.block([[ct, -st], [st, ct]])
    f32 = lambda a: np.asarray(a, np.float32)
    return f32(fwd), f32(inv), f32(c), f32(-s)


def _hyena_deltas():
    min_decay = math.log(HYENA_DECAY_TARGET) / HYENA_DECAY_SLOW
    max_decay = math.log(HYENA_DECAY_TARGET) / HYENA_DECAY_FAST
    return jnp.abs(jnp.linspace(min_decay, max_decay, D_HYENA, dtype=F32))


def _pos_feats(length, grid):
    d = jnp.arange(length, dtype=jnp.int32)
    t = d.astype(F32) / length
    bands = jnp.arange(1, N_BANDS + 1, dtype=F32)
    ang = 2.0 * math.pi * t[:, None] * bands[None, :]
    feats = [t[:, None], jnp.sin(ang), jnp.cos(ang)]
    if grid:
        rows = max(length // GRID_W, 1)
        col_ang = 2.0 * math.pi * (d % GRID_W).astype(F32) / GRID_W
        row = (d // GRID_W).astype(F32) / rows
        feats += [jnp.sin(col_ang)[:, None], jnp.cos(col_ang)[:, None], row[:, None]]
    else:
        feats.append(jnp.zeros((length, 3), F32))
    z = jnp.concatenate(feats, axis=-1)
    return jnp.pad(z, ((0, 0), (0, LANES - POS_DIM)))


def _ada_kernel(cv_ref, w_ref, b_ref, o_ref):
    s = _silu(cv_ref[...])
    o_ref[...] = _dot(s, w_ref[...]) + b_ref[...]


def _ada_mods(cv, ada_w, ada_b):
    tn = D_MODEL
    return pl.pallas_call(
        _ada_kernel,
        out_shape=jax.ShapeDtypeStruct((DEPTH, 16, 6 * D_MODEL), F32),
        grid=(DEPTH, 6 * D_MODEL // tn),
        in_specs=[
            pl.BlockSpec((16, D_MODEL), lambda l, j: (0, 0)),
            pl.BlockSpec((None, D_MODEL, tn), lambda l, j: (l, 0, j)),
            pl.BlockSpec((None, 1, tn), lambda l, j: (l, 0, j)),
        ],
        out_specs=pl.BlockSpec((None, 16, tn), lambda l, j: (l, 0, j)),
        compiler_params=_cparams(("arbitrary", "arbitrary")),
        name="ada_mods",
    )(cv, ada_w, ada_b.reshape(DEPTH, 1, 6 * D_MODEL))


def _in_kernel(x_ref, prev_ref, next_ref, g_ref, mod_ref, w_ref, sw_ref, sb_ref, cw_ref, cb_ref,
               u_ref, x2_ref, xr_ref, pg_ref, *, batch, tm):
    i = pl.program_id(0)
    last = pl.num_programs(0) - 1
    lead = 2 * batch
    xe = jnp.concatenate([prev_ref[...], x_ref[...], next_ref[...]], axis=0)
    h = _modulate(_rms(xe, g_ref[...]), mod_ref[:, 0:D_MODEL], mod_ref[:, D_MODEL:2 * D_MODEL])
    h = jnp.concatenate([h[:lead] * (i > 0).astype(F32), h[lead:lead + tm],
                         h[lead + tm:] * (i < last).astype(F32)], axis=0)
    p = jnp.dot(h.astype(BF16), w_ref[...], preferred_element_type=F32)

    def tap(d, c0, c1):
        return p[lead + d * batch:lead + d * batch + tm, c0:c1]

    def hyena_branch(j):
        c0, c1 = j * D_HYENA, (j + 1) * D_HYENA
        acc = sb_ref[:, c0:c1] + tap(-1, c0, c1) * sw_ref[0:1, c0:c1]
        acc = acc + tap(0, c0, c1) * sw_ref[1:2, c0:c1]
        return acc + tap(1, c0, c1) * sw_ref[2:3, c0:c1]

    u_ref[...] = hyena_branch(0) * hyena_branch(1)
    x2_ref[...] = hyena_branch(2)
    c0, c1 = 3 * D_HYENA, 3 * D_HYENA + D_LRU
    acc = cb_ref[...] + tap(-2, c0, c1) * cw_ref[0:1, :]
    acc = acc + tap(-1, c0, c1) * cw_ref[1:2, :]
    acc = acc + tap(0, c0, c1) * cw_ref[2:3, :]
    xr_ref[...] = acc + tap(1, c0, c1) * cw_ref[3:4, :]
    pg_ref[...] = tap(0, c1, c1 + D_LRU)


def _in_proj_convs(x, norm_mix, mods, w_in_bf, short_w, short_b, conv_w, conv_b, l, seg, batch, tm=512):
    rows = x.shape[0]
    per_prev = tm // (2 * batch)
    per_next = tm // batch
    n_next = rows // batch
    out = jax.ShapeDtypeStruct((rows, D_HYENA), F32)
    return pl.pallas_call(
        functools.partial(_in_kernel, batch=batch, tm=tm),
        out_shape=(out, out, out, out),
        grid=(rows // tm,),
        in_specs=[
            pl.BlockSpec((tm, D_MODEL), lambda i: (i, 0)),
            pl.BlockSpec((2 * batch, D_MODEL), lambda i: (jnp.maximum(i * per_prev - 1, 0), 0)),
            pl.BlockSpec((batch, D_MODEL), lambda i: (jnp.minimum((i + 1) * per_next, n_next - 1), 0)),
            pl.BlockSpec((None, 1, D_MODEL), lambda i: (l, 0, 0)),
            pl.BlockSpec((None, SUBLANES, 2 * D_MODEL), lambda i: (l, seg, 0)),
            pl.BlockSpec((None, D_MODEL, N_IN), lambda i: (l, 0, 0)),
            pl.BlockSpec((None, 3, 3 * D_HYENA), lambda i: (l, 0, 0)),
            pl.BlockSpec((None, 1, 3 * D_HYENA), lambda i: (l, 0, 0)),
            pl.BlockSpec((None, 4, D_LRU), lambda i: (l, 0, 0)),
            pl.BlockSpec((None, 1, D_LRU), lambda i: (l, 0, 0)),
        ],
        out_specs=(pl.BlockSpec((tm, D_HYENA), lambda i: (i, 0)),) * 4,
        compiler_params=_cparams(("arbitrary",)),
        name="in_proj_convs",
    )(x, x, x, norm_mix.reshape(DEPTH, 1, D_MODEL), mods, w_in_bf,
      short_w, short_b.reshape(DEPTH, 1, 3 * D_HYENA), conv_w, conv_b.reshape(DEPTH, 1, D_LRU))


def _lru_gates(xr, w_ref, ba, bx, sp):
    xb = xr.astype(BF16)
    half = D_LRU // 2
    ra, rx = [], []
    for hf in range(2):
        xs = xb[:, hf * half:(hf + 1) * half]
        ra.append(jnp.dot(xs, w_ref[0, hf], preferred_element_type=F32))
        rx.append(jnp.dot(xs, w_ref[1, hf], preferred_element_type=F32))
    ta = jnp.tanh(jnp.concatenate(ra, axis=1) + ba)
    tx = jnp.tanh(jnp.concatenate(rx, axis=1) + bx)
    c1 = (0.5 * LRU_C) * sp
    neg_log_a = c1 + c1 * ta
    a = jnp.exp2(neg_log_a * (-1.0 / math.log(2.0)))
    mult = jnp.sqrt(jnp.maximum(jnp.tanh(neg_log_a) * (1.0 + a * a), 0.0))
    return a, (0.5 * xr) * (1.0 + tx) * mult


def _softplus(x):
    return jnp.maximum(x, 0.0) + jnp.log1p(jnp.exp(-jnp.abs(x)))


def _lru_kernel(*refs, batch, tm, chunk, reverse):
    if reverse:
        (xr_ref, w_ref, ba_ref, bx_ref, lam_ref, h0_ref, hf_ref, pg_ref,
         y_ref, hl_ref, a_scr, b_scr, h_scr) = refs
    else:
        (xr_ref, w_ref, ba_ref, bx_ref, lam_ref, h0_ref,
         y_ref, hl_ref, a_scr, b_scr, h_scr) = refs
    i = pl.program_id(0)

    @pl.when(i == 0)
    def _():
        h_scr[...] = h0_ref[...].astype(F32)

    sp = _softplus(-lam_ref[...])
    ba = 0.5 * ba_ref[...]
    bx = 0.5 * bx_ref[...]

    def fill(c, carry):
        r = pl.multiple_of(c * chunk, chunk)
        a, b = _lru_gates(xr_ref[pl.ds(r, chunk), :], w_ref, ba, bx, sp)
        a_scr[pl.ds(r, chunk), :] = a
        b_scr[pl.ds(r, chunk), :] = b
        return carry

    lax.fori_loop(0, tm // chunk, fill, 0)

    steps = tm // batch

    def step(s, h):
        t = (steps - 1 - s) if reverse else s
        r = pl.multiple_of(t * batch, batch)
        h = a_scr[pl.ds(r, batch), :] * h + b_scr[pl.ds(r, batch), :]
        y_ref[pl.ds(r, batch), :] = h
        return h

    h = lax.fori_loop(0, steps, step, h_scr[...], unroll=8)
    h_scr[...] = h
    if reverse:
        y_ref[...] = (hf_ref[...] + y_ref[...]) * _gelu_tanh(pg_ref[...])

    @pl.when(i == pl.num_programs(0) - 1)
    def _():
        hl_ref[...] = h


def _lru_scan(xr, wbd, lru_ba, lru_bx, lru_lambda, h0, h0_block, l, direction, batch,
              hf=None, pg=None, tm=2048, chunk=256):
    rows = xr.shape[0]
    nt = rows // tm
    reverse = direction == 1
    row_map = (lambda i: (nt - 1 - i, 0)) if reverse else (lambda i: (i, 0))
    vec = lambda: pl.BlockSpec((None, None, 1, D_LRU), lambda i: (l, direction, 0, 0))
    in_specs = [
        pl.BlockSpec((tm, D_LRU), row_map),
        pl.BlockSpec((None, None, 2, 2, D_LRU // 2, D_LRU // 2), lambda i: (l, direction, 0, 0, 0, 0)),
        vec(), vec(), vec(),
        pl.BlockSpec((batch, D_LRU), lambda i: h0_block),
    ]
    args = [xr, wbd, lru_ba.reshape(DEPTH, 2, 1, D_LRU), lru_bx.reshape(DEPTH, 2, 1, D_LRU),
            lru_lambda.reshape(DEPTH, 2, 1, D_LRU), h0]
    if reverse:
        in_specs += [pl.BlockSpec((tm, D_LRU), row_map), pl.BlockSpec((tm, D_LRU), row_map)]
        args += [hf, pg]
    return pl.pallas_call(
        functools.partial(_lru_kernel, batch=batch, tm=tm, chunk=chunk, reverse=reverse),
        out_shape=(jax.ShapeDtypeStruct((rows, D_LRU), F32), jax.ShapeDtypeStruct((batch, D_LRU), F32)),
        grid=(nt,),
        in_specs=in_specs,
        out_specs=(pl.BlockSpec((tm, D_LRU), row_map), pl.BlockSpec((batch, D_LRU), lambda i: (0, 0))),
        scratch_shapes=[pltpu.VMEM((tm, D_LRU), F32), pltpu.VMEM((tm, D_LRU), F32),
                        pltpu.VMEM((batch, D_LRU), F32)],
        compiler_params=_cparams(("arbitrary",)),
        name="lru_rev" if reverse else "lru_fwd",
    )(*args)


def _filter_kernel(z_ref, w1_ref, b1_ref, w2_ref, b2_ref, w3_ref, b3_ref, fr_ref, dl_ref,
                   ks_ref, kd_ref, sum_ref):
    i = pl.program_id(0)
    hi = lax.Precision.HIGHEST
    z = z_ref[...]
    h1 = jnp.sin(fr_ref[0:1, :] * (jnp.dot(z, w1_ref[...], precision=hi, preferred_element_type=F32)
                                   + b1_ref[...]))
    h2 = jnp.sin(fr_ref[1:2, :] * (jnp.dot(h1, w2_ref[...], precision=hi, preferred_element_type=F32)
                                   + b2_ref[...]))
    k = jnp.dot(h2, w3_ref[...], precision=hi, preferred_element_type=F32) + b3_ref[...]
    window = jnp.exp(-z[:, 0:1] * dl_ref[...]) + HYENA_SHIFT
    kf = k[:, :D_HYENA] * window
    kb = k[:, D_HYENA:] * window
    row = lax.broadcasted_iota(jnp.int32, kb.shape, 0) + i * kb.shape[0]
    kb = jnp.where(row == 0, 0.0, kb)
    ks_ref[...] = kf + kb
    kd_ref[...] = kf - kb
    part = jnp.sum(jnp.abs(kf) + jnp.abs(kb), axis=0, keepdims=True)

    @pl.when(i == 0)
    def _():
        sum_ref[...] = part

    @pl.when(i > 0)
    def _():
        sum_ref[...] += part


def _hyena_filter_taps(length, grid, p, l, tl=256, n2_major=False):
    z = _pos_feats(length, grid)
    if n2_major:
        z = z.reshape(length // FFT_N2, FFT_N2, LANES).transpose(1, 0, 2).reshape(length, LANES)
    w1 = jnp.pad(p['filt_w1'], ((0, 0), (0, LANES - POS_DIM), (0, 0)))
    tap = jax.ShapeDtypeStruct((length, D_HYENA), F32)
    h = FILTER_HIDDEN
    return pl.pallas_call(
        _filter_kernel,
        out_shape=(tap, tap, jax.ShapeDtypeStruct((1, D_HYENA), F32)),
        grid=(length // tl,),
        in_specs=[
            pl.BlockSpec((tl, LANES), lambda i: (i, 0)),
            pl.BlockSpec((None, LANES, h), lambda i: (l, 0, 0)),
            pl.BlockSpec((None, 1, h), lambda i: (l, 0, 0)),
            pl.BlockSpec((None, h, h), lambda i: (l, 0, 0)),
            pl.BlockSpec((None, 1, h), lambda i: (l, 0, 0)),
            pl.BlockSpec((None, h, 2 * D_HYENA), lambda i: (l, 0, 0)),
            pl.BlockSpec((None, 1, 2 * D_HYENA), lambda i: (l, 0, 0)),
            pl.BlockSpec((None, 2, h), lambda i: (l, 0, 0)),
            pl.BlockSpec((1, D_HYENA), lambda i: (0, 0)),
        ],
        out_specs=(pl.BlockSpec((tl, D_HYENA), lambda i: (i, 0)),
                   pl.BlockSpec((tl, D_HYENA), lambda i: (i, 0)),
                   pl.BlockSpec((1, D_HYENA), lambda i: (0, 0))),
        compiler_params=_cparams(("arbitrary",)),
        name="hyena_filter",
    )(z, w1, p['filt_b1'].reshape(DEPTH, 1, h), p['filt_w2'], p['filt_b2'].reshape(DEPTH, 1, h),
      p['filt_w3'], p['filt_b3'].reshape(DEPTH, 1, 2 * D_HYENA), p['filt_freq'],
      _hyena_deltas().reshape(1, D_HYENA))


def _spec_direct_kernel(c_ref, s_ref, ks_ref, kd_ref, sum_ref, o_ref):
    scale = 1.0 / (sum_ref[...] + EPS)
    o_ref[0] = _dot(c_ref[...], ks_ref[...], x3=True) * scale
    o_ref[1] = _dot(s_ref[...], kd_ref[...], x3=True) * scale


def _filter_spectrum_direct(ks, kd, ksum):
    length = ks.shape[0]
    n = 2 * length
    _, _, c, ms = _dft_consts_direct(length)
    return pl.pallas_call(
        _spec_direct_kernel,
        out_shape=jax.ShapeDtypeStruct((2, n, D_HYENA), F32),
        compiler_params=_cparams(()),
        name="filter_spectrum_direct",
    )(jnp.asarray(c), jnp.asarray(ms), ks, kd, ksum)


def _conv_direct_kernel(u_ref, x2_ref, kh_ref, fwd_ref, inv_ref, bias_ref, o_ref, *, length, batch):
    n = 2 * length
    kr, ki = kh_ref[0], kh_ref[1]
    for q in range(batch // 2):
        z = jnp.concatenate([u_ref[:, 2 * q, :], u_ref[:, 2 * q + 1, :]], axis=0)
        x = _dot(fwd_ref[...], z, x3=True)
        xr, xi = x[:n], x[n:]
        y = jnp.concatenate([xr * kr - xi * ki, xr * ki + xi * kr], axis=0)
        conv = _dot(inv_ref[...], y, x3=True)
        o_ref[:, 2 * q, :] = conv[:length]
        o_ref[:, 2 * q + 1, :] = conv[length:]
    o_ref[...] = x2_ref[...] * (o_ref[...] + u_ref[...] * bias_ref[...])


def _hyena_conv_direct(u, x2, khat, hyena_bias, l, batch, cw=256):
    rows = u.shape[0]
    length = rows // batch
    fwd, inv, _, _ = _dft_consts_direct(length)
    blk = pl.BlockSpec((length, batch, cw), lambda j: (0, 0, j))
    out = pl.pallas_call(
        functools.partial(_conv_direct_kernel, length=length, batch=batch),
        out_shape=jax.ShapeDtypeStruct((length, batch, D_HYENA), F32),
        grid=(D_HYENA // cw,),
        in_specs=[blk, blk,
                  pl.BlockSpec((2, 2 * length, cw), lambda j: (0, 0, j)),
                  pl.BlockSpec(fwd.shape, lambda j: (0, 0)),
                  pl.BlockSpec(inv.shape, lambda j: (0, 0)),
                  pl.BlockSpec((None, 1, cw), lambda j: (l, 0, j))],
        out_specs=blk,
        compiler_params=_cparams(("arbitrary",)),
        name="hyena_conv_direct",
    )(u.reshape(length, batch, D_HYENA), x2.reshape(length, batch, D_HYENA), khat,
      jnp.asarray(fwd), jnp.asarray(inv), hyena_bias.reshape(DEPTH, 1, D_HYENA))
    return out.reshape(rows, D_HYENA)


def _stage1_kernel(x_ref, m_ref, o_ref, *, batch):
    m = m_ref[...].astype(BF16)
    for j in range(N2_BLOCK):
        for q in range(batch // 2):
            z = jnp.concatenate([x_ref[:, j, 2 * q, :], x_ref[:, j, 2 * q + 1, :]], axis=0)
            o_ref[j, q] = jnp.dot(m, z.astype(BF16), preferred_element_type=F32)


def _fft_stage1(u, batch, cw=256):
    c = u.shape[1]
    half1 = FFT_N1 // 2
    pairs = batch // 2
    m1, _, _, _ = _dft_consts()
    return pl.pallas_call(
        functools.partial(_stage1_kernel, batch=batch),
        out_shape=jax.ShapeDtypeStruct((FFT_N2, pairs, 2 * FFT_N1, c), F32),
        grid=(FFT_N2 // N2_BLOCK, c // cw),
        in_specs=[pl.BlockSpec((half1, N2_BLOCK, batch, cw), lambda j, k: (0, j, 0, k)),
                  pl.BlockSpec(m1.shape, lambda j, k: (0, 0))],
        out_specs=pl.BlockSpec((N2_BLOCK, pairs, 2 * FFT_N1, cw), lambda j, k: (j, 0, 0, k)),
        compiler_params=_cparams(("arbitrary", "arbitrary")),
        name="fft_stage1",
    )(u.reshape(half1, FFT_N2, batch, c), jnp.asarray(m1))


def _stage1_filter_kernel(ks_ref, kd_ref, m_ref, o_ref):
    m = m_ref[:, :FFT_N1 // 2]
    for j in range(N2_BLOCK):
        for g, ref in enumerate((ks_ref, kd_ref)):
            o_ref[j, g] = _dot(m, ref[j], x3=True)


def _stage2_filter_kernel(a_ref, g_ref, sum_ref, o_ref):
    scale = 1.0 / (sum_ref[...] + EPS)
    res = []
    for j in range(K1_BLOCK):
        bs = jnp.concatenate([a_ref[:, 0, 0, j, :], a_ref[:, 0, 1, j, :]], axis=0)
        bd = jnp.concatenate([a_ref[:, 1, 0, j, :], a_ref[:, 1, 1, j, :]], axis=0)
        res.append((_dot(g_ref[j], bs, x3=True), _dot(g_ref[j], bd, x3=True)))
    for j in range(K1_BLOCK):
        o_ref[0, j] = res[j][0][:FFT_N2] * scale
        o_ref[1, j] = res[j][1][FFT_N2:] * scale


def _filter_spectrum_2stage(ks, kd, ksum):
    c = D_HYENA
    half1 = FFT_N1 // 2
    m1, g2, _, _ = _dft_consts()
    tap = pl.BlockSpec((N2_BLOCK, half1, c), lambda j: (j, 0, 0))
    a = pl.pallas_call(
        _stage1_filter_kernel,
        out_shape=jax.ShapeDtypeStruct((FFT_N2, 2, 2 * FFT_N1, c), F32),
        grid=(FFT_N2 // N2_BLOCK,),
        in_specs=[tap, tap, pl.BlockSpec(m1.shape, lambda j: (0, 0))],
        out_specs=pl.BlockSpec((N2_BLOCK, 2, 2 * FFT_N1, c), lambda j: (j, 0, 0, 0)),
        compiler_params=_cparams(("arbitrary",)),
        name="filter_spectrum_stage1",
    )(ks.reshape(FFT_N2, half1, c), kd.reshape(FFT_N2, half1, c), jnp.asarray(m1))
    return pl.pallas_call(
        _stage2_filter_kernel,
        out_shape=jax.ShapeDtypeStruct((2, FFT_N1, FFT_N2, c), F32),
        grid=(FFT_N1 // K1_BLOCK,),
        in_specs=[pl.BlockSpec((FFT_N2, 2, 2, K1_BLOCK, c), lambda i: (0, 0, 0, i, 0)),
                  pl.BlockSpec((K1_BLOCK, 2 * FFT_N2, 2 * FFT_N2), lambda i: (i, 0, 0)),
                  pl.BlockSpec((1, c), lambda i: (0, 0))],
        out_specs=pl.BlockSpec((2, K1_BLOCK, FFT_N2, c), lambda i: (0, i, 0, 0)),
        compiler_params=_cparams(("arbitrary",)),
        name="filter_spectrum_stage2",
    )(a.reshape(FFT_N2, 2, 2, FFT_N1, c), jnp.asarray(g2), ksum)


def _stage2_kernel(a_ref, kh_ref, g_ref, gt_ref, o_ref):
    xs = []
    for j in range(K1_BLOCK):
        b = jnp.concatenate([a_ref[:, 0, j, :], a_ref[:, 1, j, :]], axis=0)
        xs.append(_dot(g_ref[j], b))
    ys = []
    for j in range(K1_BLOCK):
        xr, xi = xs[j][:FFT_N2], xs[j][FFT_N2:]
        kr, ki = kh_ref[0, j], kh_ref[1, j]
        ys.append(jnp.concatenate([xr * kr - xi * ki, xr * ki + xi * kr], axis=0).astype(BF16))
    for j in range(K1_BLOCK):
        o_ref[j] = jnp.dot(gt_ref[j].astype(BF16), ys[j], preferred_element_type=F32)


def _fft_stage2(a, khat):
    _, g2, g2t, _ = _dft_consts()
    c = D_HYENA
    pairs = a.shape[1]
    gblk = pl.BlockSpec((K1_BLOCK, 2 * FFT_N2, 2 * FFT_N2), lambda i, q: (i, 0, 0))
    return pl.pallas_call(
        _stage2_kernel,
        out_shape=jax.ShapeDtypeStruct((FFT_N1, pairs, 2 * FFT_N2, c), F32),
        grid=(FFT_N1 // K1_BLOCK, pairs),
        in_specs=[pl.BlockSpec((FFT_N2, None, 2, K1_BLOCK, c), lambda i, q: (0, q, 0, i, 0)),
                  pl.BlockSpec((2, K1_BLOCK, FFT_N2, c), lambda i, q: (0, i, 0, 0)), gblk, gblk],
        out_specs=pl.BlockSpec((K1_BLOCK, None, 2 * FFT_N2, c), lambda i, q: (i, q, 0, 0)),
        compiler_params=_cparams(("arbitrary", "arbitrary")),
        name="fft_stage2",
    )(a.reshape(FFT_N2, pairs, 2, FFT_N1, c), khat, jnp.asarray(g2), jnp.asarray(g2t))


def _stage3_kernel(a_ref, m_ref, u_ref, x2_ref, bias_ref, o_ref, *, batch):
    half1 = FFT_N1 // 2
    m = m_ref[...].astype(BF16)
    for j in range(N2_BLOCK):
        b = jnp.concatenate([a_ref[:, q, ri, j, :] for q in range(batch // 2) for ri in range(2)], axis=0)
        y = jnp.dot(m, b.astype(BF16), preferred_element_type=F32)
        o_ref[:, j] = y.reshape(half1, batch, y.shape[-1])
    o_ref[...] = x2_ref[...] * (o_ref[...] + u_ref[...] * bias_ref[...])


def _fft_stage3(a, u, x2, hyena_bias, l, batch, cw=256):
    rows, c = u.shape
    half1 = FFT_N1 // 2
    pairs = batch // 2
    m3 = _inverse_stage1_interleaved(batch)
    blk = pl.BlockSpec((half1, N2_BLOCK, batch, cw), lambda j, k: (0, j, 0, k))
    out = pl.pallas_call(
        functools.partial(_stage3_kernel, batch=batch),
        out_shape=jax.ShapeDtypeStruct((half1, FFT_N2, batch, c), F32),
        grid=(FFT_N2 // N2_BLOCK, c // cw),
        in_specs=[pl.BlockSpec((FFT_N1, pairs, 2, N2_BLOCK, cw), lambda j, k: (0, 0, 0, j, k)),
                  pl.BlockSpec(m3.shape, lambda j, k: (0, 0)),
                  blk, blk,
                  pl.BlockSpec((None, 1, cw), lambda j, k: (l, 0, k))],
        out_specs=blk,
        compiler_params=_cparams(("arbitrary", "arbitrary")),
        name="fft_stage3",
    )(a.reshape(FFT_N1, pairs, 2, FFT_N2, c), jnp.asarray(m3), u.reshape(half1, FFT_N2, batch, c),
      x2.reshape(half1, FFT_N2, batch, c), hyena_bias.reshape(DEPTH, 1, c))
    return out.reshape(rows, c)


def _hyena_conv_2stage(u, x2, khat, hyena_bias, l, batch):
    a = _fft_stage1(u, batch)
    a = _fft_stage2(a, khat)
    return _fft_stage3(a, u, x2, hyena_bias, l, batch)


def _mix_and_prenorm(hy_ref, lr_ref, x_ref, w_ref, g_ref, mod_ref):
    d = D_MODEL
    mixed = (jnp.dot(hy_ref[...].astype(BF16), w_ref[0:D_HYENA, :], preferred_element_type=F32)
             + jnp.dot(lr_ref[...].astype(BF16), w_ref[D_HYENA:, :], preferred_element_type=F32))
    rows = mixed.shape[0]
    g1 = mod_ref[:, 2 * d:3 * d]
    x = x_ref[...] + (mixed.reshape(rows // SUBLANES, SUBLANES, d) * g1[None]).reshape(rows, d)
    h = _modulate(_rms(x, g_ref[...]), mod_ref[:, 3 * d:4 * d], mod_ref[:, 4 * d:5 * d])
    return x, h.astype(BF16)


def _out_kernel(hy_ref, lr_ref, x_ref, w_ref, g_ref, mod_ref, xo_ref, h_ref):
    xo_ref[...], h_ref[...] = _mix_and_prenorm(hy_ref, lr_ref, x_ref, w_ref, g_ref, mod_ref)


def _out_proj(hy, lr, x, w_out_bf, norm_ffn, mods, l, seg, tm=512):
    rows = x.shape[0]
    d = D_MODEL
    return pl.pallas_call(
        _out_kernel,
        out_shape=(jax.ShapeDtypeStruct((rows, d), F32), jax.ShapeDtypeStruct((rows, d), BF16)),
        grid=(rows // tm,),
        in_specs=[
            pl.BlockSpec((tm, D_HYENA), lambda i: (i, 0)),
            pl.BlockSpec((tm, D_LRU), lambda i: (i, 0)),
            pl.BlockSpec((tm, d), lambda i: (i, 0)),
            pl.BlockSpec((None, D_HYENA + D_LRU, d), lambda i: (l, 0, 0)),
            pl.BlockSpec((None, 1, d), lambda i: (l, 0, 0)),
            pl.BlockSpec((None, SUBLANES, 6 * d), lambda i: (l, seg, 0)),
        ],
        out_specs=(pl.BlockSpec((tm, d), lambda i: (i, 0)), pl.BlockSpec((tm, d), lambda i: (i, 0))),
        compiler_params=_cparams(("arbitrary",)),
        name="out_proj",
    )(hy, lr, x, w_out_bf, norm_ffn.reshape(DEPTH, 1, d), mods)


def _gate_residual(x, acc, g2):
    rows, d = acc.shape
    return x + (acc.reshape(rows // SUBLANES, SUBLANES, d) * g2[None]).reshape(rows, d)


def _swiglu_chunk(h, wgu_ref, wd_ref):
    gu = jnp.dot(h, wgu_ref[...], preferred_element_type=F32)
    fc = gu.shape[1] // 2
    return jnp.dot((_silu(gu[:, :fc]) * gu[:, fc:]).astype(BF16), wd_ref[...], preferred_element_type=F32)


def _chunked_gate_up(w_gu, fc):
    parts = []
    for f in range(D_FF // fc):
        parts += [w_gu[..., f * fc:(f + 1) * fc], w_gu[..., D_FF + f * fc:D_FF + (f + 1) * fc]]
    return jnp.concatenate(parts, axis=-1)


def _mix_ffn_kernel(hy_ref, lr_ref, x_ref, w_ref, g_ref, mod_ref, wgu_ref, wd_ref,
                    o_ref, x_scr, h_scr, acc_ref):
    f = pl.program_id(1)

    @pl.when(f == 0)
    def _():
        x_scr[...], h_scr[...] = _mix_and_prenorm(hy_ref, lr_ref, x_ref, w_ref, g_ref, mod_ref)

    part = _swiglu_chunk(h_scr[...], wgu_ref, wd_ref)

    @pl.when(f == 0)
    def _():
        acc_ref[...] = part

    @pl.when(f > 0)
    def _():
        acc_ref[...] += part

    @pl.when(f == pl.num_programs(1) - 1)
    def _():
        o_ref[...] = _gate_residual(x_scr[...], acc_ref[...], mod_ref[:, 5 * D_MODEL:])


def _out_proj_ffn_dense(hy, lr, x, w_out_bf, norm_ffn, w_gu_bf, w_down_bf, mods, m, l, seg, tm=512, fc=FFN_CHUNK):
    rows = x.shape[0]
    d = D_MODEL
    nf = D_FF // fc
    return pl.pallas_call(
        _mix_ffn_kernel,
        out_shape=jax.ShapeDtypeStruct((rows, d), F32),
        grid=(rows // tm, nf),
        in_specs=[
            pl.BlockSpec((tm, D_HYENA), lambda i, f: (i, 0)),
            pl.BlockSpec((tm, D_LRU), lambda i, f: (i, 0)),
            pl.BlockSpec((tm, d), lambda i, f: (i, 0)),
            pl.BlockSpec((None, D_HYENA + D_LRU, d), lambda i, f: (l, 0, 0)),
            pl.BlockSpec((None, 1, d), lambda i, f: (l, 0, 0)),
            pl.BlockSpec((None, SUBLANES, 6 * d), lambda i, f: (l, seg, 0)),
            pl.BlockSpec((None, d, 2 * fc), lambda i, f: (m, 0, f)),
            pl.BlockSpec((None, fc, d), lambda i, f: (m, f, 0)),
        ],
        out_specs=pl.BlockSpec((tm, d), lambda i, f: (i, 0)),
        scratch_shapes=[pltpu.VMEM((tm, d), F32), pltpu.VMEM((tm, d), BF16), pltpu.VMEM((tm, d), F32)],
        compiler_params=_cparams(("arbitrary", "arbitrary")),
        name="out_proj_ffn_dense",
    )(hy, lr, x, w_out_bf, norm_ffn.reshape(DEPTH, 1, d), mods, w_gu_bf, w_down_bf)


def _route_top2(logits):
    lane = lax.broadcasted_iota(jnp.int32, logits.shape, 1)
    neg = jnp.float32(-jnp.inf)
    v = jnp.where(lane < N_EXPERTS, logits, neg)
    m1 = jnp.max(v, axis=-1, keepdims=True)
    i1 = jnp.min(jnp.where(v == m1, lane, LANES), axis=-1, keepdims=True)
    v2 = jnp.where(lane == i1, neg, v)
    m2 = jnp.max(v2, axis=-1, keepdims=True)
    i2 = jnp.min(jnp.where(v2 == m2, lane, LANES), axis=-1, keepdims=True)
    e2 = jnp.exp(m2 - m1)
    w1 = 1.0 / (1.0 + e2)
    w2 = e2 / (1.0 + e2)
    first = lane == i1
    second = lane == i2
    comb = jnp.where(first, w1, 0.0) + jnp.where(second, w2, 0.0)
    return comb, jnp.where(first | second, 1.0, 0.0)


MOE_TILE = 2048
MOE_CH = 256
MOE_SB = 128
MOE_WIN = 256
MOE_CS_ROWS = 16


def _router_kernel(h_ref, r_ref, comb_ref, key_ref, keyt_ref, cs_ref, *, tt):
    nch = tt // MOE_CH
    r_hi, r_lo = _split(r_ref[...])
    h = h_ref[...]
    logits = (jnp.dot(h, r_hi, preferred_element_type=F32) + jnp.dot(h, r_lo, preferred_element_type=F32))
    comb, sel = _route_top2(logits)
    comb_ref[...] = comb
    rr = lax.broadcasted_iota(jnp.int32, (MOE_CH, MOE_CH), 0)
    cc = lax.broadcasted_iota(jnp.int32, (MOE_CH, MOE_CH), 1)
    strict_lower = jnp.where(cc < rr, 1.0, 0.0).astype(BF16)
    cs_ref[...] = jnp.zeros(cs_ref.shape, jnp.int32)
    carry = jnp.zeros((1, LANES), F32)
    for c in range(nch):
        selc = sel[c * MOE_CH:(c + 1) * MOE_CH]
        cs_ref[c:c + 1, :] = carry.astype(jnp.int32)
        rank = jnp.dot(strict_lower, selc.astype(BF16), preferred_element_type=F32) + carry
        key_ref[c * MOE_CH:(c + 1) * MOE_CH, :] = jnp.where(selc > 0.0, rank, -1.0)
        carry = carry + jnp.sum(selc, axis=0, keepdims=True)
    cs_ref[nch:nch + 1, :] = carry.astype(jnp.int32)
    keyt = jnp.transpose(key_ref[...])
    for e in range(N_EXPERTS):
        for c in range(nch):
            keyt_ref[e, c] = keyt[e:e + 1, c * MOE_CH:(c + 1) * MOE_CH]


def _moe_route(h, router_pad, m, tt):
    rows, d = h.shape
    nt = rows // tt
    nch = tt // MOE_CH
    tok = jax.ShapeDtypeStruct((rows, LANES), F32)
    return pl.pallas_call(
        functools.partial(_router_kernel, tt=tt),
        out_shape=(tok, tok,
                   jax.ShapeDtypeStruct((N_EXPERTS, rows // MOE_CH, 1, MOE_CH), F32),
                   jax.ShapeDtypeStruct((nt * MOE_CS_ROWS, LANES), jnp.int32)),
        grid=(nt,),
        in_specs=[pl.BlockSpec((tt, d), lambda i: (i, 0)),
                  pl.BlockSpec((None, d, LANES), lambda i: (m, 0, 0))],
        out_specs=(pl.BlockSpec((tt, LANES), lambda i: (i, 0)),
                   pl.BlockSpec((tt, LANES), lambda i: (i, 0)),
                   pl.BlockSpec((N_EXPERTS, nch, 1, MOE_CH), lambda i: (0, i, 0, 0)),
                   pl.BlockSpec((MOE_CS_ROWS, LANES), lambda i: (i, 0))),
        compiler_params=_cparams(("arbitrary",)),
        name="moe_route",
    )(h, router_pad)


def _moe_kernel(cs_ref, h_ref, x_ref, comb_ref, key_ref, keyt_ref, wg_ref, wu_ref, wd_ref, mod_ref,
                o_ref, xg_ref, y_ref, wt_ref, *, tt):
    i = pl.program_id(0)
    e = pl.program_id(1)
    f = pl.program_id(2)
    last_f = pl.num_programs(2) - 1
    nch = tt // MOE_CH
    main = wg_ref.shape[1] - LANES
    wt_ref[...] = jnp.concatenate([wg_ref[:, main:], wu_ref[:, main:]], axis=1)
    base = i * ((nch + 1) * N_EXPERTS) + e

    def count_before(c):
        return cs_ref[base + c * N_EXPERTS]

    n_sub = (count_before(nch) + (MOE_SB - 1)) // MOE_SB

    def windows(c):
        lo = count_before(c)
        hi = count_before(c + 1)
        w0 = (lo // MOE_SB) * MOE_SB
        return w0, (hi - w0 + (MOE_WIN - 1)) // MOE_WIN

    def row_block(r0, size):
        return pl.ds(pl.multiple_of(r0, MOE_SB), size)

    @pl.when((e == 0) & (f == 0))
    def _():
        o_ref[...] = jnp.zeros_like(o_ref)

    @pl.when(f == 0)
    def _():
        def clear(j, carry):
            xg_ref[row_block(j * MOE_SB, MOE_SB), :] = jnp.zeros((MOE_SB, D_MODEL), BF16)
            return carry

        lax.fori_loop(0, n_sub + MOE_WIN // MOE_SB, clear, 0)
        y_ref[row_block(n_sub * MOE_SB, MOE_WIN), :] = jnp.zeros((MOE_WIN, D_MODEL), BF16)

        def gather(c, r0):
            row = (lax.broadcasted_iota(jnp.int32, (MOE_WIN, MOE_CH), 0) + r0).astype(F32)
            sel = jnp.where(keyt_ref[e, c] == row, 1.0, 0.0).astype(BF16)
            part = jnp.dot(sel, h_ref[c * MOE_CH:(c + 1) * MOE_CH, :], preferred_element_type=F32)
            xg_ref[row_block(r0, MOE_WIN), :] += part.astype(BF16)

        starts = [windows(c) for c in range(nch)]
        for c in range(nch):
            gather(c, starts[c][0])
        for c in range(nch):
            w0, n_win = starts[c]
            lax.fori_loop(1, n_win, lambda k, carry, c=c, w0=w0: (gather(c, w0 + k * MOE_WIN), carry)[1], 0)

    def expert_rows(r0, size):
        rows = row_block(r0, size)
        xg = xg_ref[rows, :]
        gm = jnp.dot(xg, wg_ref[:, :main], preferred_element_type=F32)
        um = jnp.dot(xg, wu_ref[:, :main], preferred_element_type=F32)
        t = jnp.dot(xg, wt_ref[...], preferred_element_type=F32)
        act = jnp.concatenate([_silu(gm) * um, _silu(t[:, :LANES]) * t[:, LANES:]], axis=1)
        y = jnp.dot(act.astype(BF16), wd_ref[...], preferred_element_type=F32)

        @pl.when(f == 0)
        def _():
            y_ref[rows, :] = y.astype(BF16)

        @pl.when(f > 0)
        def _():
            y_ref[rows, :] = (y_ref[rows, :].astype(F32) + y).astype(BF16)

    n_pair = n_sub // 2
    lax.fori_loop(0, n_pair, lambda j, carry: (expert_rows(j * (2 * MOE_SB), 2 * MOE_SB), carry)[1], 0)

    @pl.when(n_sub % 2 == 1)
    def _():
        expert_rows(n_pair * (2 * MOE_SB), MOE_SB)

    @pl.when(f == last_f)
    def _():
        lane_e = lax.broadcasted_iota(jnp.int32, (MOE_CH, LANES), 1) == e

        def scatter(c, r0, key, w):
            pos = (lax.broadcasted_iota(jnp.int32, (MOE_CH, MOE_WIN), 1) + r0).astype(F32)
            sel = jnp.where(key == pos, 1.0, 0.0).astype(BF16)
            part = jnp.dot(sel, y_ref[row_block(r0, MOE_WIN), :], preferred_element_type=F32)
            o_ref[c * MOE_CH:(c + 1) * MOE_CH, :] += w * part

        starts = [windows(c) for c in range(nch)]
        cols = []
        for c in range(nch):
            tok = slice(c * MOE_CH, (c + 1) * MOE_CH)
            key = jnp.sum(jnp.where(lane_e, key_ref[tok, :], 0.0), axis=-1, keepdims=True)
            w = jnp.sum(jnp.where(lane_e, comb_ref[tok, :], 0.0), axis=-1, keepdims=True)
            cols.append((key, w))
            scatter(c, starts[c][0], key, w)
        for c in range(nch):
            w0, n_win = starts[c]
            key, w = cols[c]
            lax.fori_loop(1, n_win,
                          lambda k, carry, c=c, w0=w0, key=key, w=w:
                          (scatter(c, w0 + k * MOE_WIN, key, w), carry)[1], 0)

    @pl.when((e == pl.num_programs(1) - 1) & (f == last_f))
    def _():
        o_ref[...] = _gate_residual(x_ref[...], o_ref[...], mod_ref[...])


def _ffn_moe(h, x, router_pad, w_gu_bf, w_down_bf, mods, m, l, seg, fc=FFN_CHUNK):
    rows = x.shape[0]
    d = D_MODEL
    nf = D_FF // fc
    tt = MOE_TILE
    nt = rows // tt
    nch = tt // MOE_CH
    comb, key, keyt, cs = _moe_route(h, router_pad, m, tt)
    counts = cs.reshape(nt, MOE_CS_ROWS, LANES)[:, :nch + 1, :N_EXPERTS].reshape(-1)
    once = pl.Buffered(1)
    grid_spec = pltpu.PrefetchScalarGridSpec(
        num_scalar_prefetch=1,
        grid=(nt, N_EXPERTS, nf),
        in_specs=[
            pl.BlockSpec((tt, d), lambda i, e, f, cs: (i, 0), pipeline_mode=once),
            pl.BlockSpec((tt, d), lambda i, e, f, cs: (i, 0), pipeline_mode=once),
            pl.BlockSpec((tt, LANES), lambda i, e, f, cs: (i, 0), pipeline_mode=once),
            pl.BlockSpec((tt, LANES), lambda i, e, f, cs: (i, 0), pipeline_mode=once),
            pl.BlockSpec((N_EXPERTS, nch, 1, MOE_CH), lambda i, e, f, cs: (0, i, 0, 0), pipeline_mode=once),
            pl.BlockSpec((None, None, d, fc), lambda i, e, f, cs: (m, e, 0, f)),
            pl.BlockSpec((None, None, d, fc), lambda i, e, f, cs: (m, e, 0, nf + f)),
            pl.BlockSpec((None, None, fc, d), lambda i, e, f, cs: (m, e, f, 0)),
            pl.BlockSpec((None, SUBLANES, d), lambda i, e, f, cs: (l, seg, 5)),
        ],
        out_specs=pl.BlockSpec((tt, d), lambda i, e, f, cs: (i, 0), pipeline_mode=once),
        scratch_shapes=[pltpu.VMEM((tt + MOE_SB + MOE_WIN, d), BF16),
                        pltpu.VMEM((tt + MOE_SB + MOE_WIN, d), BF16),
                        pltpu.VMEM((d, 2 * LANES), BF16)],
    )
    return pl.pallas_call(
        functools.partial(_moe_kernel, tt=tt),
        out_shape=jax.ShapeDtypeStruct((rows, d), F32),
        grid_spec=grid_spec,
        compiler_params=_cparams(("arbitrary", "arbitrary", "arbitrary")),
        name="ffn_moe",
    )(counts, h, x, comb, key, keyt, w_gu_bf, w_gu_bf, w_down_bf, mods)


def _final_kernel(x_ref, g_ref, o_ref, *, batch):
    for b in range(batch):
        o_ref[b] = _rms(x_ref[:, b, :], g_ref[...])


def _final_norm(x, norm_final, batch, tl=64):
    rows, d = x.shape
    length = rows // batch
    return pl.pallas_call(
        functools.partial(_final_kernel, batch=batch),
        out_shape=jax.ShapeDtypeStruct((batch, length, d), F32),
        grid=(length // tl,),
        in_specs=[pl.BlockSpec((tl, batch, d), lambda i: (i, 0, 0)),
                  pl.BlockSpec((1, d), lambda i: (0, 0))],
        out_specs=pl.BlockSpec((batch, tl, d), lambda i: (0, i, 0)),
        compiler_params=_cparams(("arbitrary",)),
        name="final_norm",
    )(x.reshape(length, batch, d), norm_final.reshape(1, d))


def _time_major_kernel(x_ref, o_ref, *, batch):
    for t in range(o_ref.shape[0]):
        o_ref[t] = x_ref[:, t, :]


def _to_time_major(x, tl=64):
    batch, length, d = x.shape
    out = pl.pallas_call(
        functools.partial(_time_major_kernel, batch=batch),
        out_shape=jax.ShapeDtypeStruct((length, batch, d), x.dtype),
        grid=(length // tl,),
        in_specs=[pl.BlockSpec((batch, tl, d), lambda i: (0, i, 0))],
        out_specs=pl.BlockSpec((tl, batch, d), lambda i: (i, 0, 0)),
        compiler_params=_cparams(("arbitrary",)),
        name="to_time_major",
    )(x)
    return out.reshape(length * batch, d)


def _block_diag_heads(w):
    hh = N_LRU_HEADS // 2
    w = w.reshape(DEPTH, 2, 2, hh, LRU_HEAD_DIM, LRU_HEAD_DIM)
    eye = jnp.eye(hh, dtype=w.dtype)
    bd = jnp.einsum('ldgpij,pq->ldgpiqj', w, eye)
    return bd.reshape(DEPTH, 2, 2, hh * LRU_HEAD_DIM, hh * LRU_HEAD_DIM)


def kernel(x_prompt, x_sample, state_lru, c, c_ctx, norm_mix, norm_ffn, norm_final, ada_w, ada_b,
           w_in, w_out, hyena_short_w, hyena_short_b, filt_w1, filt_b1, filt_w2, filt_b2, filt_w3,
           filt_b3, filt_freq, hyena_bias, lru_conv_w, lru_conv_b, lru_wa, lru_ba, lru_wx, lru_bx,
           lru_lambda, ffn_w_gu, ffn_w_down, moe_router, moe_w_gu, moe_w_down):
    bp, lp, d = x_prompt.shape
    bs, ls, _ = x_sample.shape
    assert bs == SUBLANES and bp % SUBLANES == 0 and ls == (FFT_N1 // 2) * FFT_N2

    xs = _to_time_major(x_sample)
    xp = _to_time_major(x_prompt)

    cv = jnp.concatenate([c, jnp.broadcast_to(c_ctx[None, :], (SUBLANES, d))], axis=0)
    mods = _ada_mods(cv, ada_w, ada_b)

    w_in_bf = w_in.astype(BF16)
    w_out_bf = w_out.astype(BF16)
    ffn_gu_bf = _chunked_gate_up(ffn_w_gu, FFN_CHUNK).astype(BF16)
    ffn_down_bf = ffn_w_down.astype(BF16)
    moe_gu_bf = moe_w_gu.astype(BF16)
    moe_down_bf = moe_w_down.astype(BF16)
    router_pad = jnp.pad(moe_router, ((0, 0), (0, 0), (0, LANES - N_EXPERTS)))
    wbd = (0.5 * jnp.stack([_block_diag_heads(lru_wa), _block_diag_heads(lru_wx)], axis=2)).astype(BF16)
    filt = dict(filt_w1=filt_w1, filt_b1=filt_b1, filt_w2=filt_w2, filt_b2=filt_b2,
                filt_w3=filt_w3, filt_b3=filt_b3, filt_freq=filt_freq)
    state_flat = state_lru.reshape(bs, DEPTH * 2 * D_LRU)
    zero_state = jnp.zeros((bp, D_LRU), F32)

    new_states = []
    for l in range(DEPTH):
        m = l // 2
        for path in range(2):
            if path == 0:
                x, batch, seg, grid_pos = xp, bp, 1, False
            else:
                x, batch, seg, grid_pos = xs, bs, 0, True
            length = x.shape[0] // batch

            u, x2, xr, pg = _in_proj_convs(x, norm_mix, mods, w_in_bf, hyena_short_w, hyena_short_b,
                                           lru_conv_w, lru_conv_b, l, seg, batch)

            ks, kd, ksum = _hyena_filter_taps(length, grid_pos, filt, l, n2_major=(path == 1))
            if path == 0:
                khat = _filter_spectrum_direct(ks, kd, ksum)
                hy = _hyena_conv_direct(u, x2, khat, hyena_bias, l, batch)
            else:
                khat = _filter_spectrum_2stage(ks, kd, ksum)
                hy = _hyena_conv_2stage(u, x2, khat, hyena_bias, l, batch)

            lru_tm = min(2048, x.shape[0])
            if path == 0:
                h0f, h0f_blk, h0r, h0r_blk = zero_state, (0, 0), zero_state, (0, 0)
            else:
                h0f, h0f_blk, h0r, h0r_blk = state_flat, (0, 2 * l), state_flat, (0, 2 * l + 1)
            hf, hlast_f = _lru_scan(xr, wbd, lru_ba, lru_bx, lru_lambda, h0f, h0f_blk, l, 0, batch,
                                    tm=lru_tm)
            lr, hlast_r = _lru_scan(xr, wbd, lru_ba, lru_bx, lru_lambda, h0r, h0r_blk, l, 1, batch,
                                    hf=hf, pg=pg, tm=lru_tm)
            if path == 0:
                new_states.append(jnp.stack([hlast_f, hlast_r], axis=1))

            if l % 2 == 0:
                x = _out_proj_ffn_dense(hy, lr, x, w_out_bf, norm_ffn, ffn_gu_bf, ffn_down_bf, mods, m, l, seg)
            else:
                x, h = _out_proj(hy, lr, x, w_out_bf, norm_ffn, mods, l, seg)
                x = _ffn_moe(h, x, router_pad, moe_gu_bf, moe_down_bf, mods, m, l, seg)
            if path == 0:
                xp = x
            else:
                xs = x

    new_state_lru = jnp.stack(new_states, axis=1).astype(state_lru.dtype)
    y_prompt = _final_norm(xp, norm_final, bp)
    y_sample = _final_norm(xs, norm_final, bs)
    return (y_prompt, y_sample, new_state_lru)
```
